```python
import jax, jax.numpy as jnp
from jax import lax
import numpy as np

D_MODEL = 1024
BATCH = 2
SEQ = 16384
DEPTH = 1
DEC_BATCH = 128
DEC_SEQ = 4
PAST_LEN = 8192
PAGE_SIZE = 128

SB_HEAD_DIM = 64
SB_WIDTH = D_MODEL // 2
SB_HEADS = SB_WIDTH // SB_HEAD_DIM
SB_BIAS_INIT = -6.0
GM_WIDTH = D_MODEL - SB_WIDTH
GM_GROUP_CH = 128
GM_GROUPS = GM_WIDTH // GM_GROUP_CH
CHUNK = 128
Q_BLOCK = 128
MIX_WIDTH = SB_WIDTH + GM_WIDTH
IN_WIDTH = 3 * SB_WIDTH + 2 * GM_WIDTH
FFN_HIDDEN = -(-(8 * D_MODEL) // (3 * 256)) * 256
RMS_EPS = 1e-6
SB_SCALE = 1.0 / float(np.sqrt(SB_HEAD_DIM))

kernel_name = "hymba_gmlp_stickbreaking_decode_step"


def rmsnorm(x, g):
    xf = x.astype(jnp.float32)
    r = lax.rsqrt(jnp.mean(xf * xf, axis=-1, keepdims=True) + RMS_EPS)
    return (xf * r * g.astype(jnp.float32)).astype(x.dtype)


def split_projection(h, w_in, gm_v_norm):
    b, t, _ = h.shape
    p = jnp.einsum('btd,de->bte', h, w_in)
    q, k, v, gu, gv = jnp.split(
        p, [SB_WIDTH, 2 * SB_WIDTH, 3 * SB_WIDTH, 3 * SB_WIDTH + GM_WIDTH], axis=-1)
    heads = lambda a: a.reshape(b, t, SB_HEADS, SB_HEAD_DIM)
    gu = jax.nn.gelu(gu).reshape(b, t, GM_GROUPS, GM_GROUP_CH)
    gv = rmsnorm(jax.nn.gelu(gv).reshape(b, t, GM_GROUPS, GM_GROUP_CH), gm_v_norm)
    return heads(q), heads(k), heads(v), gu, gv


def stick_breaking_weights(z, sb_bias, q_pos, k_pos):
    z = z.astype(jnp.float32) + sb_bias.astype(jnp.float32)[None, :, None, None]
    mask = k_pos[None, :] < q_pos[:, None]
    log1m = jnp.where(mask, -jax.nn.softplus(z), 0.0)
    between = lax.cumsum(log1m, axis=z.ndim - 1, reverse=True) - log1m
    return jnp.where(mask, jnp.exp(jax.nn.log_sigmoid(z) + between), 0.0)


def stick_breaking_prompt(q, k, v, sb_bias):
    b, s, h, d = q.shape
    nb = s // Q_BLOCK
    qb = jnp.transpose(q.reshape(b, nb, Q_BLOCK, h, d), (1, 0, 2, 3, 4))
    k_pos = jnp.arange(s)

    def one_block(args):
        i, qi = args
        z = jnp.einsum('bqhd,bkhd->bhqk', qi, k) * SB_SCALE
        a = stick_breaking_weights(z, sb_bias, i * Q_BLOCK + jnp.arange(Q_BLOCK), k_pos)
        return jnp.einsum('bhqk,bkhd->bqhd', a.astype(v.dtype), v)

    o = lax.map(one_block, (jnp.arange(nb), qb))
    return jnp.transpose(o, (1, 0, 2, 3, 4)).reshape(b, s, h * d)


def stick_breaking_sample(q, k_new, v_new, cache_k, cache_v, page_table, sb_bias):
    bd, t, h, d = q.shape
    past = page_table.shape[1] * cache_k.shape[1]
    k_past = cache_k[page_table].reshape(bd, past, h, d)
    v_past = cache_v[page_table].reshape(bd, past, h, d)
    z = jnp.concatenate([jnp.einsum('bqhd,bkhd->bhqk', q, k_past),
                         jnp.einsum('bqhd,bkhd->bhqk', q, k_new)], axis=-1) * SB_SCALE
    a = stick_breaking_weights(z, sb_bias, past + jnp.arange(t), jnp.arange(past + t)).astype(v_new.dtype)
    o = (jnp.einsum('bhqk,bkhd->bqhd', a[..., :past], v_past)
         + jnp.einsum('bhqk,bkhd->bqhd', a[..., past:], v_new))
    return o.reshape(bd, t, h * d)


def spatial_gate(gu, gv, gm_ws, gm_b):
    t = gv.shape[2]
    ws = gm_ws[:, :t, :t] * jnp.tril(jnp.ones((t, t), gm_ws.dtype))
    s = jnp.einsum('gts,bnsgc->bntgc', ws, gv) + jnp.transpose(gm_b[:, :t])[:, :, None]
    return gu * s


def merge_and_ffn(x, attn_o, gm_o, attn_out_norm, gm_out_norm, w_o, mix_post_norm,
                  ffn_pre_norm, w_ffn_in, w_ffn_out, ffn_post_norm):
    mix = jnp.concatenate([rmsnorm(attn_o, attn_out_norm), rmsnorm(gm_o, gm_out_norm)], axis=-1)
    h = x + rmsnorm(jnp.einsum('bte,ed->btd', mix, w_o), mix_post_norm)
    gate, up = jnp.split(jnp.einsum('btd,df->btf', rmsnorm(h, ffn_pre_norm), w_ffn_in), 2, axis=-1)
    f = jnp.einsum('btf,fd->btd', jax.nn.silu(gate) * up, w_ffn_out)
    return h + rmsnorm(f, ffn_post_norm)


def setup_inputs(seed: int = 0) -> dict:
    key = jax.random.key(seed)
    ks = jax.random.split(key, 20)
    n_pages = PAST_LEN // PAGE_SIZE
    n_used = DEC_BATCH * n_pages
    n_phys = n_used + n_used // 4
    nrm = lambda k, shape, scale=1.0: scale * jax.random.normal(k, shape, jnp.float32)
    gain = lambda k, shape: 1.0 + 0.05 * jax.random.normal(k, shape, jnp.float32)
    page_table = jax.random.permutation(ks[4], n_phys)[:n_used].astype(jnp.int32).reshape(DEC_BATCH, n_pages)
    return {
        "x_prompt": nrm(ks[0], (BATCH, SEQ, D_MODEL)),
        "x_sample": nrm(ks[1], (DEC_BATCH, DEC_SEQ, D_MODEL)),
        "cache_k": nrm(ks[2], (DEPTH, n_phys, PAGE_SIZE, SB_HEADS, SB_HEAD_DIM)),
        "cache_v": nrm(ks[3], (DEPTH, n_phys, PAGE_SIZE, SB_HEADS, SB_HEAD_DIM)),
        "page_table": page_table,
        "mix_pre_norm": gain(ks[5], (DEPTH, D_MODEL)),
        "w_in": nrm(ks[6], (DEPTH, D_MODEL, IN_WIDTH), D_MODEL ** -0.5),
        "sb_bias": SB_BIAS_INIT + 0.1 * jax.random.normal(ks[18], (DEPTH, SB_HEADS), jnp.float32),
        "gm_v_norm": gain(ks[7], (DEPTH, GM_GROUPS, GM_GROUP_CH)),
        "gm_ws": nrm(ks[8], (DEPTH, GM_GROUPS, CHUNK, CHUNK), CHUNK ** -0.5),
        "gm_b": gain(ks[9], (DEPTH, GM_GROUPS, CHUNK)),
        "attn_out_norm": gain(ks[10], (DEPTH, SB_WIDTH)),
        "gm_out_norm": gain(ks[11], (DEPTH, GM_WIDTH)),
        "w_o": nrm(ks[12], (DEPTH, MIX_WIDTH, D_MODEL), MIX_WIDTH ** -0.5),
        "mix_post_norm": gain(ks[13], (DEPTH, D_MODEL)),
        "ffn_pre_norm": gain(ks[14], (DEPTH, D_MODEL)),
        "w_ffn_in": nrm(ks[15], (DEPTH, D_MODEL, 2 * FFN_HIDDEN), D_MODEL ** -0.5),
        "w_ffn_out": nrm(ks[16], (DEPTH, FFN_HIDDEN, D_MODEL), FFN_HIDDEN ** -0.5),
        "ffn_post_norm": gain(ks[17], (DEPTH, D_MODEL)),
    }


def reference(x_prompt, x_sample, cache_k, cache_v, page_table, mix_pre_norm, w_in, sb_bias,
              gm_v_norm, gm_ws, gm_b, attn_out_norm, gm_out_norm, w_o, mix_post_norm,
              ffn_pre_norm, w_ffn_in, w_ffn_out, ffn_post_norm):
    xp, xs = x_prompt, x_sample
    kp_l, vp_l, ksn_l, vsn_l, gvs_l = [], [], [], [], []
    for l in range(DEPTH):
        post = lambda x, a, g: merge_and_ffn(
            x, a, g, attn_out_norm[l], gm_out_norm[l], w_o[l], mix_post_norm[l],
            ffn_pre_norm[l], w_ffn_in[l], w_ffn_out[l], ffn_post_norm[l])

        b, s, _ = xp.shape
        q, k, v, gu, gv = split_projection(rmsnorm(xp, mix_pre_norm[l]), w_in[l], gm_v_norm[l])
        attn_p = stick_breaking_prompt(q, k, v, sb_bias[l])
        nch = s // CHUNK
        gm_p = spatial_gate(gu.reshape(b, nch, CHUNK, GM_GROUPS, GM_GROUP_CH),
                            gv.reshape(b, nch, CHUNK, GM_GROUPS, GM_GROUP_CH),
                            gm_ws[l], gm_b[l]).reshape(b, s, GM_WIDTH)
        xp_next = post(xp, attn_p, gm_p)
        kp_l.append(k)
        vp_l.append(v)

        bd, t, _ = xs.shape
        qs, ks_, vs_, gus, gvs = split_projection(rmsnorm(xs, mix_pre_norm[l]), w_in[l], gm_v_norm[l])
        attn_s = stick_breaking_sample(qs, ks_, vs_, cache_k[l], cache_v[l], page_table, sb_bias[l])
        gm_s = spatial_gate(gus[:, None], gvs[:, None], gm_ws[l], gm_b[l]).reshape(bd, t, GM_WIDTH)
        xs_next = post(xs, attn_s, gm_s)
        ksn_l.append(ks_)
        vsn_l.append(vs_)
        gvs_l.append(gvs.reshape(bd, t, GM_WIDTH))

        xp, xs = xp_next, xs_next
    return (xp, xs, jnp.stack(kp_l), jnp.stack(vp_l), jnp.stack(ksn_l), jnp.stack(vsn_l), jnp.stack(gvs_l))
```

```python
import functools

import jax
import jax.numpy as jnp
from jax import lax
from jax.experimental import pallas as pl
from jax.experimental.pallas import tpu as pltpu

F32 = jnp.float32
BF16 = jnp.bfloat16

RMS_EPS = 1e-6
LOG2E = 1.4426950408889634
HEAD_DIM = 64
GM_GROUP_CH = 128
CHUNK = 128
LANES = 128
ATTN_BLOCK = 256
PROJ_ROWS = 512
FFN_COLS = 256
SAMPLE_KEY_BLOCK = 256
VMEM_LIMIT_BYTES = 56 * 1024 * 1024

_NT = (((1,), (1,)), ((), ()))


def _log2(n):
    assert n > 0 and n & (n - 1) == 0, n
    return n.bit_length() - 1


def _rmsnorm(x, g):
    r = lax.rsqrt(jnp.mean(x * x, axis=-1, keepdims=True) + RMS_EPS)
    return x * r * g


def _gelu_tanh(x):
    return 0.5 * x * (1.0 + jnp.tanh(0.7978845608028654 * (x + 0.044715 * (x * x * x))))


def _stick_terms(z):
    e = jnp.exp2(-jnp.abs(z))
    l2 = jnp.log2(1.0 + e)
    logsig = jnp.minimum(z, 0.0) - l2
    return logsig, logsig - z


def _resident(block_shape, index_map):
    return pl.BlockSpec(block_shape, index_map, pipeline_mode=pl.Buffered(1))


def _proj_prompt_kernel(x_ref, gpre_ref, wnat_ref, wqvt_ref, gvn_ref, ws_ref, bt_ref,
                        k_ref, v_ref, kb_ref, qt_ref, vt_ref, gm_ref, *, q_scale):
    tm = x_ref.shape[1]
    sbw = k_ref.shape[2]
    gmw = gm_ref.shape[2]
    hn = _rmsnorm(x_ref[0], gpre_ref[...]).astype(BF16)
    p = jnp.dot(hn, wnat_ref[...], preferred_element_type=F32)
    k = p[:, :sbw]
    k_ref[0] = k
    kb_ref[0] = k.astype(BF16)
    v_ref[0] = p[:, sbw:2 * sbw]
    pt = lax.dot_general(wqvt_ref[...], hn, _NT, preferred_element_type=F32)
    qt = (pt[:sbw] * q_scale).astype(BF16)
    vt = pt[sbw:].astype(BF16)
    for s in range(tm // ATTN_BLOCK):
        qt_ref[0, s] = qt[:, s * ATTN_BLOCK:(s + 1) * ATTN_BLOCK]
        vt_ref[0, s] = vt[:, s * ATTN_BLOCK:(s + 1) * ATTN_BLOCK]
    gu = _gelu_tanh(p[:, 2 * sbw:2 * sbw + gmw])
    gv = _gelu_tanh(p[:, 2 * sbw + gmw:])
    for g in range(gmw // GM_GROUP_CH):
        cols = slice(g * GM_GROUP_CH, (g + 1) * GM_GROUP_CH)
        gvn = _rmsnorm(gv[:, cols], gvn_ref[:, cols]).astype(BF16)
        for n in range(tm // CHUNK):
            rows = slice(n * CHUNK, (n + 1) * CHUNK)
            s_ = jnp.dot(ws_ref[g], gvn[rows], preferred_element_type=F32) + bt_ref[:, g:g + 1]
            gm_ref[0, rows, cols] = gu[rows, cols] * s_


def _proj_prompt(x, gpre, wnat, wqvt, gvn, ws_tril, bt, q_scale):
    b, s, d = x.shape
    sbw = wqvt.shape[0] // 2
    gmw = gvn.shape[1]
    tm = PROJ_ROWS
    nsub = tm // ATTN_BLOCK
    nblk = s // ATTN_BLOCK
    row_spec = lambda w: pl.BlockSpec((1, tm, w), lambda bi, i: (bi, i, 0))
    t_spec = pl.BlockSpec((1, nsub, sbw, ATTN_BLOCK), lambda bi, i: (bi, i, 0, 0))
    const = lambda shape: pl.BlockSpec(shape, lambda bi, i: (0,) * len(shape))
    return pl.pallas_call(
        functools.partial(_proj_prompt_kernel, q_scale=q_scale),
        grid=(b, s // tm),
        in_specs=[row_spec(d), const(gpre.shape), const(wnat.shape), const(wqvt.shape),
                  const(gvn.shape), const(ws_tril.shape), const(bt.shape)],
        out_specs=[row_spec(sbw), row_spec(sbw), row_spec(sbw), t_spec, t_spec, row_spec(gmw)],
        out_shape=[jax.ShapeDtypeStruct((b, s, sbw), F32), jax.ShapeDtypeStruct((b, s, sbw), F32),
                   jax.ShapeDtypeStruct((b, s, sbw), BF16),
                   jax.ShapeDtypeStruct((b, nblk, sbw, ATTN_BLOCK), BF16),
                   jax.ShapeDtypeStruct((b, nblk, sbw, ATTN_BLOCK), BF16),
                   jax.ShapeDtypeStruct((b, s, gmw), F32)],
        compiler_params=pltpu.CompilerParams(dimension_semantics=("arbitrary", "arbitrary"),
                                             vmem_limit_bytes=VMEM_LIMIT_BYTES),
        name="proj_prompt",
    )(x, gpre, wnat, wqvt, gvn, ws_tril, bt)


def _attn_prompt_kernel(bias_ref, qt_ref, kb_ref, vt_ref, o_ref, wq_ref, acc_ref, carry_ref):
    t = ATTN_BLOCK
    heads = carry_ref.shape[0]
    i = pl.program_id(1)
    acc_ref[...] = jnp.zeros_like(acc_ref)
    carry_ref[...] = jnp.zeros_like(carry_ref)
    zeros = jnp.zeros((HEAD_DIM, t), BF16)
    for h in range(heads):
        qh = qt_ref[0, 0, h * HEAD_DIM:(h + 1) * HEAD_DIM, :]
        wq_ref[h] = jnp.concatenate([qh, zeros] if h % 2 == 0 else [zeros, qh], axis=0)
    row = lax.broadcasted_iota(jnp.int32, (t, t), 0)
    col = lax.broadcasted_iota(jnp.int32, (t, t), 1)
    later = col > row
    suffix = later.astype(BF16)

    def block(j, masked):
        for h in range(heads):
            pair = h // 2
            kp = kb_ref[0, pl.ds(pl.multiple_of(j * t, t), t), pair * LANES:(pair + 1) * LANES]
            z = jnp.dot(kp, wq_ref[h], preferred_element_type=F32) + bias_ref[h]
            logsig, log1m = _stick_terms(z)
            if masked:
                log1m = jnp.where(later, log1m, 0.0)
            lb = log1m.astype(BF16)
            between = jnp.dot(suffix, lb, preferred_element_type=F32)
            a = jnp.exp2(logsig + between)
            if masked:
                a = jnp.where(later, a, 0.0)
            vth = vt_ref[0, j, h * HEAD_DIM:(h + 1) * HEAD_DIM, :]
            p = jnp.dot(vth, a.astype(BF16), preferred_element_type=F32)
            c = carry_ref[h:h + 1, :]
            rows = slice(h * HEAD_DIM, (h + 1) * HEAD_DIM)
            acc_ref[rows, :] = acc_ref[rows, :] + p * jnp.exp2(c)
            carry_ref[h:h + 1, :] = c + between[0:1, :] + lb[0:1, :].astype(F32)

    block(i, True)

    def body(n, carry):
        block(i - 1 - n, False)
        return carry

    lax.fori_loop(0, i, body, 0)
    o_ref[0] = acc_ref[...].T


def _attn_prompt(bias2, qt, kb, vt):
    b, nblk, sbw, t = qt.shape
    s = kb.shape[1]
    heads = sbw // HEAD_DIM
    return pl.pallas_call(
        _attn_prompt_kernel,
        grid=(b, nblk),
        in_specs=[pl.BlockSpec(memory_space=pltpu.SMEM),
                  pl.BlockSpec((1, 1, sbw, t), lambda bi, i: (bi, i, 0, 0)),
                  _resident((1, s, sbw), lambda bi, i: (bi, 0, 0)),
                  _resident((1, nblk, sbw, t), lambda bi, i: (bi, 0, 0, 0))],
        out_specs=pl.BlockSpec((1, t, sbw), lambda bi, i: (bi, i, 0)),
        out_shape=jax.ShapeDtypeStruct((b, s, sbw), F32),
        scratch_shapes=[pltpu.VMEM((heads, 2 * HEAD_DIM, t), BF16),
                        pltpu.VMEM((sbw, t), F32),
                        pltpu.VMEM((heads, t), F32)],
        compiler_params=pltpu.CompilerParams(dimension_semantics=("arbitrary", "arbitrary"),
                                             vmem_limit_bytes=VMEM_LIMIT_BYTES),
        name="attn_prompt",
    )(bias2, qt, kb, vt)


def _proj_sample_kernel(x_ref, gpre_ref, w_ref, gvn_ref, wsk_ref, bcol_ref,
                        q_ref, k_ref, v_ref, gvs_ref, gm_ref, *, q_scale):
    sbw = k_ref.shape[1]
    gmw = gm_ref.shape[1]
    hn = _rmsnorm(x_ref[...], gpre_ref[...]).astype(BF16)
    p = jnp.dot(hn, w_ref[...], preferred_element_type=F32)
    q_ref[...] = (p[:, :sbw] * q_scale).astype(BF16)
    k_ref[...] = p[:, sbw:2 * sbw]
    v_ref[...] = p[:, 2 * sbw:3 * sbw]
    gu = _gelu_tanh(p[:, 3 * sbw:3 * sbw + gmw])
    gv = _gelu_tanh(p[:, 3 * sbw + gmw:])
    for g in range(gmw // GM_GROUP_CH):
        cols = slice(g * GM_GROUP_CH, (g + 1) * GM_GROUP_CH)
        gvn = _rmsnorm(gv[:, cols], gvn_ref[:, cols])
        gvs_ref[:, cols] = gvn
        s_ = jnp.dot(wsk_ref[g], gvn.astype(BF16), preferred_element_type=F32) + bcol_ref[:, g:g + 1]
        gm_ref[:, cols] = gu[:, cols] * s_


def _proj_sample(x, gpre, w, gvn, ws_kron, bcol, q_scale):
    rows, _ = x.shape
    sbw = (w.shape[1] - 2 * gvn.shape[1]) // 3
    gmw = gvn.shape[1]
    vmem = pl.BlockSpec(memory_space=pltpu.VMEM)
    return pl.pallas_call(
        functools.partial(_proj_sample_kernel, q_scale=q_scale),
        in_specs=[vmem] * 6,
        out_specs=[vmem] * 5,
        out_shape=[jax.ShapeDtypeStruct((rows, sbw), BF16), jax.ShapeDtypeStruct((rows, sbw), F32),
                   jax.ShapeDtypeStruct((rows, sbw), F32), jax.ShapeDtypeStruct((rows, gmw), F32),
                   jax.ShapeDtypeStruct((rows, gmw), F32)],
        compiler_params=pltpu.CompilerParams(vmem_limit_bytes=VMEM_LIMIT_BYTES),
        name="proj_sample",
    )(x, gpre, w, gvn, ws_kron, bcol)


def _attn_sample_kernel(pt_ref, q_ref, kn_ref, vn_ref, bias_ref, *rest, pages_per_step):
    del pt_ref
    k_refs = rest[:pages_per_step]
    v_refs = rest[pages_per_step:2 * pages_per_step]
    o_ref, knew_ref, vnew_ref, acc_ref, carry_ref = rest[2 * pages_per_step:]
    tpos, sbw = q_ref.shape[1], q_ref.shape[2]
    heads = sbw // HEAD_DIM
    rows = tpos * heads
    page = k_refs[0].shape[1]
    kb = SAMPLE_KEY_BLOCK
    c = pl.program_id(1)

    r_i = lax.broadcasted_iota(jnp.int32, (rows, sbw), 0)
    c_i = lax.broadcasted_iota(jnp.int32, (rows, sbw), 1)
    own_head = (c_i >> _log2(HEAD_DIM)) == (r_i & (heads - 1))
    q = q_ref[0].astype(F32)
    qrep = jnp.concatenate([jnp.broadcast_to(q[t:t + 1, :], (heads, sbw)) for t in range(tpos)], axis=0)
    qbd = jnp.where(own_head, qrep, 0.0).astype(BF16)

    def suffix_and_total(width):
        j = lax.broadcasted_iota(jnp.int32, (width, width), 0)
        s = lax.broadcasted_iota(jnp.int32, (width, width), 1)
        return jnp.concatenate([(j > s).astype(BF16), jnp.ones((width, kb), BF16)], axis=1)

    def sweep(kblk, vblk, valid):
        width = kblk.shape[0]
        z = lax.dot_general(qbd, kblk, _NT, preferred_element_type=F32) + bias_ref[:, :width]
        logsig, log1m = _stick_terms(z)
        if valid is not None:
            log1m = jnp.where(valid, log1m, 0.0)
        sums = jnp.dot(log1m.astype(BF16), suffix_and_total(width), preferred_element_type=F32)
        carry = carry_ref[...]
        a = jnp.exp2(logsig + sums[:, :width] + carry[:, :width])
        if valid is not None:
            a = jnp.where(valid, a, 0.0)
        acc_ref[...] += jnp.dot(a.astype(BF16), vblk, preferred_element_type=F32)
        carry_ref[...] = carry + sums[:, width:]

    @pl.when(c == 0)
    def _():
        acc_ref[...] = jnp.zeros_like(acc_ref)
        carry_ref[...] = jnp.zeros_like(carry_ref)
        knew_ref[...] = jnp.zeros_like(knew_ref)
        vnew_ref[...] = jnp.zeros_like(vnew_ref)
        knew_ref[0:tpos, :] = kn_ref[0]
        vnew_ref[0:tpos, :] = vn_ref[0]
        key = lax.broadcasted_iota(jnp.int32, (rows, LANES), 1)
        qpos = lax.broadcasted_iota(jnp.int32, (rows, LANES), 0) >> _log2(heads)
        sweep(knew_ref[...].astype(BF16), vnew_ref[...].astype(BF16), key < qpos)

    pages_per_block = kb // page
    for blk in reversed(range(pages_per_step // pages_per_block)):
        sel = range(blk * pages_per_block, (blk + 1) * pages_per_block)
        kblk = jnp.concatenate([k_refs[n][0] for n in sel], axis=0).astype(BF16)
        vblk = jnp.concatenate([v_refs[n][0] for n in sel], axis=0).astype(BF16)
        sweep(kblk, vblk, None)

    @pl.when(c == pl.num_programs(1) - 1)
    def _():
        p = jnp.where(own_head, acc_ref[...], 0.0)
        o_ref[0] = jnp.concatenate(
            [jnp.sum(p[t * heads:(t + 1) * heads, :], axis=0, keepdims=True) for t in range(tpos)], axis=0)


def _attn_sample(page_table, q, k_new, v_new, bias_rows, cache_k, cache_v):
    nb, tpos, sbw = q.shape
    n_pages = page_table.shape[1]
    page = cache_k.shape[1]
    heads = sbw // HEAD_DIM
    rows = tpos * heads
    pps = 16 if n_pages % 16 == 0 else n_pages
    nsteps = n_pages // pps
    per_req = lambda w: pl.BlockSpec((1, tpos, w), lambda b, c, pt: (b, 0, 0))

    def page_spec(n):
        return pl.BlockSpec(
            (1, page, sbw),
            lambda b, c, pt, n=n: (pt[b * n_pages + (nsteps - 1 - c) * pps + n], 0, 0))

    grid_spec = pltpu.PrefetchScalarGridSpec(
        num_scalar_prefetch=1,
        grid=(nb, nsteps),
        in_specs=[per_req(sbw), per_req(sbw), per_req(sbw),
                  pl.BlockSpec(bias_rows.shape, lambda b, c, pt: (0, 0))]
                 + [page_spec(n) for n in range(pps)] * 2,
        out_specs=per_req(sbw),
        scratch_shapes=[pltpu.VMEM((LANES, sbw), F32), pltpu.VMEM((LANES, sbw), F32),
                        pltpu.VMEM((rows, sbw), F32), pltpu.VMEM((rows, SAMPLE_KEY_BLOCK), F32)],
    )
    return pl.pallas_call(
        functools.partial(_attn_sample_kernel, pages_per_step=pps),
        grid_spec=grid_spec,
        out_shape=jax.ShapeDtypeStruct((nb, tpos, sbw), F32),
        compiler_params=pltpu.CompilerParams(dimension_semantics=("arbitrary", "arbitrary"),
                                             vmem_limit_bytes=VMEM_LIMIT_BYTES),
        name="attn_sample",
    )(page_table.reshape(-1), q, k_new, v_new, bias_rows, *([cache_k] * pps), *([cache_v] * pps))


def _post_kernel(x_ref, a_ref, gm_ref, gan_ref, ggn_ref, wo_ref, gpost_ref, gpre2_ref,
                 w1_ref, w2_ref, gpost2_ref, y_ref, act_ref):
    hidden = w2_ref.shape[0]
    mix = jnp.concatenate([_rmsnorm(a_ref[...], gan_ref[...]),
                           _rmsnorm(gm_ref[...], ggn_ref[...])], axis=-1).astype(BF16)
    m = jnp.dot(mix, wo_ref[...], preferred_element_type=F32)
    h = x_ref[...] + _rmsnorm(m, gpost_ref[...])
    hn = _rmsnorm(h, gpre2_ref[...]).astype(BF16)
    for j in range(hidden // FFN_COLS):
        lo = j * FFN_COLS
        gate = jnp.dot(hn, w1_ref[:, lo:lo + FFN_COLS], preferred_element_type=F32)
        up = jnp.dot(hn, w1_ref[:, hidden + lo:hidden + lo + FFN_COLS], preferred_element_type=F32)
        act_ref[:, lo:lo + FFN_COLS] = (gate * jax.nn.sigmoid(gate) * up).astype(BF16)
    f = jnp.dot(act_ref[...], w2_ref[...], preferred_element_type=F32)
    y_ref[...] = h + _rmsnorm(f, gpost2_ref[...])


def _post(x, attn_o, gm_o, gan, ggn, wo, gpost, gpre2, w1, w2, gpost2):
    n, d = x.shape
    tm = min(PROJ_ROWS, n)
    hidden = w2.shape[0]
    row_spec = lambda w: pl.BlockSpec((tm, w), lambda i: (i, 0))
    const = lambda a: _resident(a.shape, lambda i: (0,) * a.ndim)
    return pl.pallas_call(
        _post_kernel,
        grid=(n // tm,),
        in_specs=[row_spec(d), row_spec(attn_o.shape[1]), row_spec(gm_o.shape[1]),
                  const(gan), const(ggn), const(wo), const(gpost), const(gpre2),
                  const(w1), const(w2), const(gpost2)],
        out_specs=row_spec(d),
        out_shape=jax.ShapeDtypeStruct((n, d), F32),
        scratch_shapes=[pltpu.VMEM((tm, hidden), BF16)],
        compiler_params=pltpu.CompilerParams(dimension_semantics=("arbitrary",),
                                             vmem_limit_bytes=VMEM_LIMIT_BYTES),
        name="post",
    )(x, attn_o, gm_o, gan, ggn, wo, gpost, gpre2, w1, w2, gpost2)


def kernel(x_prompt, x_sample, cache_k, cache_v, page_table, mix_pre_norm, w_in, sb_bias, gm_v_norm,
           gm_ws, gm_b, attn_out_norm, gm_out_norm, w_o, mix_post_norm, ffn_pre_norm, w_ffn_in,
           w_ffn_out, ffn_post_norm):
    depth = w_in.shape[0]
    b, s, d = x_prompt.shape
    nb, tpos, _ = x_sample.shape
    heads = cache_k.shape[3]
    sbw = heads * cache_k.shape[4]
    gmw = gm_v_norm.shape[1] * gm_v_norm.shape[2]
    groups = gm_ws.shape[1]
    q_scale = LOG2E / float(cache_k.shape[4]) ** 0.5
    row = lambda v: v.reshape(1, -1)

    xp, xs = x_prompt, x_sample.reshape(nb * tpos, d)
    kp_l, vp_l, ksn_l, vsn_l, gvs_l = [], [], [], [], []
    for l in range(depth):
        w = w_in[l]
        w_bf = w.astype(BF16)
        wqvt = jnp.concatenate([w[:, :sbw], w[:, 2 * sbw:3 * sbw]], axis=1).T.astype(BF16)
        ws = gm_ws[l]
        tril = jnp.tril(jnp.ones(ws.shape[1:], ws.dtype))
        ws_tril = (ws * tril).astype(BF16)
        ws_open = (ws * tril)[:, :tpos, :tpos]
        ws_kron = jnp.stack([jnp.kron(jnp.eye(nb, dtype=ws.dtype), ws_open[g]) for g in range(groups)]).astype(BF16)
        bcol = jnp.tile(gm_b[l][:, :tpos].T, (nb, 1))
        bias2 = sb_bias[l] * LOG2E
        bias_rows = jnp.broadcast_to(jnp.tile(bias2, tpos)[:, None], (tpos * heads, SAMPLE_KEY_BLOCK))
        gvn = row(gm_v_norm[l])
        post = lambda x, a, g: _post(
            x, a, g, row(attn_out_norm[l]), row(gm_out_norm[l]), w_o[l].astype(BF16), row(mix_post_norm[l]),
            row(ffn_pre_norm[l]), w_ffn_in[l].astype(BF16), w_ffn_out[l].astype(BF16), row(ffn_post_norm[l]))

        k_p, v_p, kb, qt, vt, gm_p = _proj_prompt(
            xp, row(mix_pre_norm[l]), w_bf[:, sbw:], wqvt, gvn, ws_tril, gm_b[l].T, q_scale)
        attn_p = _attn_prompt(bias2, qt, kb, vt)
        xp = post(xp.reshape(b * s, d), attn_p.reshape(b * s, sbw), gm_p.reshape(b * s, gmw)).reshape(b, s, d)
        kp_l.append(k_p.reshape(b, s, heads, -1))
        vp_l.append(v_p.reshape(b, s, heads, -1))

        q_s, k_s, v_s, gvs, gm_s = _proj_sample(xs, row(mix_pre_norm[l]), w_bf, gvn, ws_kron, bcol, q_scale)
        attn_s = _attn_sample(
            page_table, q_s.reshape(nb, tpos, sbw), k_s.reshape(nb, tpos, sbw), v_s.reshape(nb, tpos, sbw),
            bias_rows, cache_k[l].reshape(-1, cache_k.shape[2], sbw), cache_v[l].reshape(-1, cache_v.shape[2], sbw))
        xs = post(xs, attn_s.reshape(nb * tpos, sbw), gm_s)
        ksn_l.append(k_s.reshape(nb, tpos, heads, -1))
        vsn_l.append(v_s.reshape(nb, tpos, heads, -1))
        gvs_l.append(gvs.reshape(nb, tpos, gmw))

    stack = lambda per_layer: per_layer[0][None] if depth == 1 else jnp.stack(per_layer)
    return (xp, xs.reshape(nb, tpos, d), stack(kp_l), stack(vp_l), stack(ksn_l), stack(vsn_l), stack(gvs_l))
```

```python
import functools

import jax
import jax.numpy as jnp
from jax import lax
from jax.experimental import pallas as pl
from jax.experimental.pallas import tpu as pltpu

F32 = jnp.float32
BF16 = jnp.bfloat16

RMS_EPS = 1e-6
LOG2E = 1.4426950408889634
HEAD_DIM = 64
GM_GROUP_CH = 128
CHUNK = 128
LANES = 128
ATTN_BLOCK = 256
PROJ_ROWS = 512
FFN_COLS = 256
SAMPLE_KEY_BLOCK = 256
SAMPLE_PAGES_PER_STEP = 16
VMEM_LIMIT_BYTES = 56 * 1024 * 1024
MASK_LOGIT = -1e30

_NT = (((1,), (1,)), ((), ()))


def _log2(n):
    assert n > 0 and n & (n - 1) == 0, n
    return n.bit_length() - 1


def _rmsnorm(x, g):
    r = lax.rsqrt(jnp.mean(x * x, axis=-1, keepdims=True) + RMS_EPS)
    return x * r * g


def _gelu_tanh(x):
    return 0.5 * x * (1.0 + jnp.tanh(0.7978845608028654 * (x + 0.044715 * (x * x * x))))


def _stick_terms(z):
    sign = jnp.uint32(0x80000000)
    neg_abs = lax.bitcast_convert_type(lax.bitcast_convert_type(z, jnp.uint32) | sign, F32)
    e = jnp.exp2(neg_abs)
    l2 = jnp.log(1.0 + e) * LOG2E
    logsig = jnp.minimum(z, 0.0) - l2
    return logsig, logsig - z


def _resident(block_shape, index_map):
    return pl.BlockSpec(block_shape, index_map, pipeline_mode=pl.Buffered(1))


def _proj_prompt_kernel(x_ref, gpre_ref, wnat_ref, wqvt_ref, gvn_ref, ws_ref, bt_ref,
                        k_ref, v_ref, kb_ref, qt_ref, vt_ref, gm_ref, *, q_scale):
    tm = x_ref.shape[1]
    sbw = k_ref.shape[2]
    gmw = gm_ref.shape[2]
    hn = _rmsnorm(x_ref[0], gpre_ref[...]).astype(BF16)
    p = jnp.dot(hn, wnat_ref[...], preferred_element_type=F32)
    k = p[:, :sbw]
    k_ref[0] = k
    kb_ref[0] = k.astype(BF16)
    v_ref[0] = p[:, sbw:2 * sbw]
    pt = lax.dot_general(wqvt_ref[...], hn, _NT, preferred_element_type=F32)
    qt = (pt[:sbw] * q_scale).astype(BF16)
    vt = pt[sbw:].astype(BF16)
    for s in range(tm // ATTN_BLOCK):
        qt_ref[0, s] = qt[:, s * ATTN_BLOCK:(s + 1) * ATTN_BLOCK]
        vt_ref[0, s] = vt[:, s * ATTN_BLOCK:(s + 1) * ATTN_BLOCK]
    gu = _gelu_tanh(p[:, 2 * sbw:2 * sbw + gmw])
    gv = _gelu_tanh(p[:, 2 * sbw + gmw:])
    for g in range(gmw // GM_GROUP_CH):
        cols = slice(g * GM_GROUP_CH, (g + 1) * GM_GROUP_CH)
        gvn = _rmsnorm(gv[:, cols], gvn_ref[:, cols]).astype(BF16)
        for n in range(tm // CHUNK):
            rows = slice(n * CHUNK, (n + 1) * CHUNK)
            s_ = jnp.dot(ws_ref[g], gvn[rows], preferred_element_type=F32) + bt_ref[:, g:g + 1]
            gm_ref[0, rows, cols] = gu[rows, cols] * s_


def _proj_prompt(x, gpre, wnat, wqvt, gvn, ws_tril, bt, q_scale):
    b, s, d = x.shape
    sbw = wqvt.shape[0] // 2
    gmw = gvn.shape[1]
    tm = PROJ_ROWS
    nsub = tm // ATTN_BLOCK
    nblk = s // ATTN_BLOCK
    row_spec = lambda w: pl.BlockSpec((1, tm, w), lambda bi, i: (bi, i, 0))
    t_spec = pl.BlockSpec((1, nsub, sbw, ATTN_BLOCK), lambda bi, i: (bi, i, 0, 0))
    const = lambda shape: pl.BlockSpec(shape, lambda bi, i: (0,) * len(shape))
    return pl.pallas_call(
        functools.partial(_proj_prompt_kernel, q_scale=q_scale),
        grid=(b, s // tm),
        in_specs=[row_spec(d), const(gpre.shape), const(wnat.shape), const(wqvt.shape),
                  const(gvn.shape), const(ws_tril.shape), const(bt.shape)],
        out_specs=[row_spec(sbw), row_spec(sbw), row_spec(sbw), t_spec, t_spec, row_spec(gmw)],
        out_shape=[jax.ShapeDtypeStruct((b, s, sbw), F32), jax.ShapeDtypeStruct((b, s, sbw), F32),
                   jax.ShapeDtypeStruct((b, s, sbw), BF16),
                   jax.ShapeDtypeStruct((b, nblk, sbw, ATTN_BLOCK), BF16),
                   jax.ShapeDtypeStruct((b, nblk, sbw, ATTN_BLOCK), BF16),
                   jax.ShapeDtypeStruct((b, s, gmw), F32)],
        compiler_params=pltpu.CompilerParams(dimension_semantics=("arbitrary", "arbitrary"),
                                             vmem_limit_bytes=VMEM_LIMIT_BYTES),
        name="proj_prompt",
    )(x, gpre, wnat, wqvt, gvn, ws_tril, bt)


def _attn_prompt_kernel(bias_ref, qt_ref, kb_ref, vt_ref, o_ref,
                        wq_ref, zoff_ref, suffix_ref, ls_ref, lb_ref, a_ref, acc_ref, used_ref):
    t = ATTN_BLOCK
    heads = wq_ref.shape[0]
    i = pl.program_id(1)

    @pl.when(i == 0)
    def _():
        row = lax.broadcasted_iota(jnp.int32, (t, t), 0)
        col = lax.broadcasted_iota(jnp.int32, (t, t), 1)
        later = col > row
        suffix_ref[...] = later.astype(BF16)
        for h in range(heads):
            zoff_ref[h, 0] = jnp.where(later, bias_ref[h], bias_ref[h] + MASK_LOGIT)
            zoff_ref[h, 1] = jnp.full((t, t), bias_ref[h], F32)

    ls_ref[...] = jnp.full(ls_ref.shape, MASK_LOGIT, F32)
    lb_ref[...] = jnp.zeros_like(lb_ref)
    a_ref[...] = jnp.zeros_like(a_ref)
    acc_ref[...] = jnp.zeros_like(acc_ref)
    used_ref[...] = jnp.zeros_like(used_ref)
    zeros = jnp.zeros((HEAD_DIM, t), BF16)
    for h in range(heads):
        qh = qt_ref[0, 0, h * HEAD_DIM:(h + 1) * HEAD_DIM, :]
        wq_ref[h] = jnp.concatenate([qh, zeros] if h % 2 == 0 else [zeros, qh], axis=0)

    def step(n, carry):
        jc = jnp.clip(i + 2 - n, 0, i)
        for h in range(heads):
            rows = slice(h * HEAD_DIM, (h + 1) * HEAD_DIM)
            p = jnp.dot(vt_ref[0, jc, rows, :], a_ref[h], preferred_element_type=F32)
            acc_ref[rows, :] = acc_ref[rows, :] + p * jnp.exp2(used_ref[1, h:h + 1, :])
        for h in range(heads):
            lb = lb_ref[h]
            between = jnp.dot(suffix_ref[...], lb, preferred_element_type=F32)
            a_ref[h] = jnp.exp2(ls_ref[h] + between).astype(BF16)
            used = used_ref[0, h:h + 1, :]
            used_ref[1, h:h + 1, :] = used
            used_ref[0, h:h + 1, :] = used + between[0:1, :] + lb[0:1, :].astype(F32)
        ja = jnp.maximum(i - n, 0)
        sel = jnp.minimum(n, 1)
        for h in range(heads):
            pair = h // 2
            kp = kb_ref[0, pl.ds(pl.multiple_of(ja * t, t), t), pair * LANES:(pair + 1) * LANES]
            z = jnp.dot(kp, wq_ref[h], preferred_element_type=F32) + zoff_ref[h, sel]
            logsig, log1m = _stick_terms(z)
            ls_ref[h] = logsig
            lb_ref[h] = log1m.astype(BF16)
        return carry

    lax.fori_loop(0, i + 3, step, 0)
    o_ref[0] = acc_ref[...].T


def _attn_prompt(bias2, qt, kb, vt):
    b, nblk, sbw, t = qt.shape
    s = kb.shape[1]
    heads = sbw // HEAD_DIM
    return pl.pallas_call(
        _attn_prompt_kernel,
        grid=(b, nblk),
        in_specs=[pl.BlockSpec(memory_space=pltpu.SMEM),
                  pl.BlockSpec((1, 1, sbw, t), lambda bi, i: (bi, i, 0, 0)),
                  _resident((1, s, sbw), lambda bi, i: (bi, 0, 0)),
                  _resident((1, nblk, sbw, t), lambda bi, i: (bi, 0, 0, 0))],
        out_specs=pl.BlockSpec((1, t, sbw), lambda bi, i: (bi, i, 0)),
        out_shape=jax.ShapeDtypeStruct((b, s, sbw), F32),
        scratch_shapes=[pltpu.VMEM((heads, 2 * HEAD_DIM, t), BF16),
                        pltpu.VMEM((heads, 2, t, t), F32),
                        pltpu.VMEM((t, t), BF16),
                        pltpu.VMEM((heads, t, t), F32),
                        pltpu.VMEM((heads, t, t), BF16),
                        pltpu.VMEM((heads, t, t), BF16),
                        pltpu.VMEM((sbw, t), F32),
                        pltpu.VMEM((2, heads, t), F32)],
        compiler_params=pltpu.CompilerParams(dimension_semantics=("arbitrary", "arbitrary"),
                                             vmem_limit_bytes=VMEM_LIMIT_BYTES),
        name="attn_prompt",
    )(bias2, qt, kb, vt)


def _proj_sample_kernel(x_ref, gpre_ref, w_ref, gvn_ref, wsk_ref, bcol_ref,
                        q_ref, k_ref, v_ref, gvs_ref, gm_ref, *, q_scale):
    sbw = k_ref.shape[1]
    gmw = gm_ref.shape[1]
    hn = _rmsnorm(x_ref[...], gpre_ref[...]).astype(BF16)
    p = jnp.dot(hn, w_ref[...], preferred_element_type=F32)
    q_ref[...] = (p[:, :sbw] * q_scale).astype(BF16)
    k_ref[...] = p[:, sbw:2 * sbw]
    v_ref[...] = p[:, 2 * sbw:3 * sbw]
    gu = _gelu_tanh(p[:, 3 * sbw:3 * sbw + gmw])
    gv = _gelu_tanh(p[:, 3 * sbw + gmw:])
    for g in range(gmw // GM_GROUP_CH):
        cols = slice(g * GM_GROUP_CH, (g + 1) * GM_GROUP_CH)
        gvn = _rmsnorm(gv[:, cols], gvn_ref[:, cols])
        gvs_ref[:, cols] = gvn
        s_ = jnp.dot(wsk_ref[g], gvn.astype(BF16), preferred_element_type=F32) + bcol_ref[:, g:g + 1]
        gm_ref[:, cols] = gu[:, cols] * s_


def _proj_sample(x, gpre, w, gvn, ws_kron, bcol, q_scale):
    rows, _ = x.shape
    sbw = (w.shape[1] - 2 * gvn.shape[1]) // 3
    gmw = gvn.shape[1]
    vmem = pl.BlockSpec(memory_space=pltpu.VMEM)
    return pl.pallas_call(
        functools.partial(_proj_sample_kernel, q_scale=q_scale),
        in_specs=[vmem] * 6,
        out_specs=[vmem] * 5,
        out_shape=[jax.ShapeDtypeStruct((rows, sbw), BF16), jax.ShapeDtypeStruct((rows, sbw), F32),
                   jax.ShapeDtypeStruct((rows, sbw), F32), jax.ShapeDtypeStruct((rows, gmw), F32),
                   jax.ShapeDtypeStruct((rows, gmw), F32)],
        compiler_params=pltpu.CompilerParams(vmem_limit_bytes=VMEM_LIMIT_BYTES),
        name="proj_sample",
    )(x, gpre, w, gvn, ws_kron, bcol)


def _attn_sample_kernel(pt_ref, q_ref, kn_ref, vn_ref, bias_ref, *rest, pages_per_step):
    del pt_ref
    k_refs = rest[:pages_per_step]
    v_refs = rest[pages_per_step:2 * pages_per_step]
    o_ref, knew_ref, vnew_ref, acc_ref, carry_ref = rest[2 * pages_per_step:]
    tpos, sbw = q_ref.shape[1], q_ref.shape[2]
    heads = sbw // HEAD_DIM
    rows = tpos * heads
    kb = SAMPLE_KEY_BLOCK
    c = pl.program_id(1)

    r_i = lax.broadcasted_iota(jnp.int32, (rows, sbw), 0)
    c_i = lax.broadcasted_iota(jnp.int32, (rows, sbw), 1)
    own_head = (c_i >> _log2(HEAD_DIM)) == (r_i & (heads - 1))
    q = q_ref[0].astype(F32)
    qrep = jnp.concatenate([jnp.broadcast_to(q[t:t + 1, :], (heads, sbw)) for t in range(tpos)], axis=0)
    qbd = jnp.where(own_head, qrep, 0.0).astype(BF16)

    def suffix_and_total(width):
        j = lax.broadcasted_iota(jnp.int32, (width, width), 0)
        s = lax.broadcasted_iota(jnp.int32, (width, width), 1)
        return jnp.concatenate([(j > s).astype(BF16), jnp.ones((width, kb), BF16)], axis=1)

    def sweep(k, v, valid, feature_major):
        width = k.shape[1] if feature_major else k.shape[0]
        blk = min(width, kb)
        if feature_major:
            z = jnp.dot(qbd, k, preferred_element_type=F32)
        else:
            z = lax.dot_general(qbd, k, _NT, preferred_element_type=F32)
        z = z + bias_ref[:, :width]
        logsig, log1m = _stick_terms(z)
        if valid is not None:
            log1m = jnp.where(valid, log1m, 0.0)
        lb = log1m.astype(BF16)
        weights = suffix_and_total(blk)
        used = carry_ref[...]
        between = [None] * (width // blk)
        for b in reversed(range(width // blk)):
            sums = jnp.dot(lb[:, b * blk:(b + 1) * blk], weights, preferred_element_type=F32)
            between[b] = sums[:, :blk] + used[:, :blk]
            used = used + sums[:, blk:]
        a = jnp.exp2(logsig + jnp.concatenate(between, axis=1))
        if valid is not None:
            a = jnp.where(valid, a, 0.0)
        if feature_major:
            acc_ref[...] += lax.dot_general(a.astype(BF16), v, _NT, preferred_element_type=F32)
        else:
            acc_ref[...] += jnp.dot(a.astype(BF16), v, preferred_element_type=F32)
        carry_ref[...] = used

    @pl.when(c == 0)
    def _():
        acc_ref[...] = jnp.zeros_like(acc_ref)
        carry_ref[...] = jnp.zeros_like(carry_ref)
        knew_ref[...] = jnp.zeros_like(knew_ref)
        vnew_ref[...] = jnp.zeros_like(vnew_ref)
        knew_ref[0:tpos, :] = kn_ref[0]
        vnew_ref[0:tpos, :] = vn_ref[0]
        key = lax.broadcasted_iota(jnp.int32, (rows, LANES), 1)
        qpos = lax.broadcasted_iota(jnp.int32, (rows, LANES), 0) >> _log2(heads)
        sweep(knew_ref[...].astype(BF16), vnew_ref[...].astype(BF16), key < qpos, False)

    sweep(jnp.concatenate([r[0, 0] for r in k_refs], axis=1).astype(BF16),
          jnp.concatenate([r[0, 0] for r in v_refs], axis=1).astype(BF16), None, True)

    @pl.when(c == pl.num_programs(1) - 1)
    def _():
        p = jnp.where(own_head, acc_ref[...], 0.0)
        o_ref[0] = jnp.concatenate(
            [jnp.sum(p[t * heads:(t + 1) * heads, :], axis=0, keepdims=True) for t in range(tpos)], axis=0)


def _attn_sample(layer, page_table, q, k_new, v_new, bias2, cache_k, cache_v):
    nb, tpos, sbw = q.shape
    n_pages = page_table.shape[1]
    _, _, page, heads, head_dim = cache_k.shape
    rows = tpos * heads
    pps = SAMPLE_PAGES_PER_STEP if n_pages % SAMPLE_PAGES_PER_STEP == 0 else n_pages
    nsteps = n_pages // pps
    per_req = lambda w: pl.BlockSpec((1, tpos, w), lambda b, c, pt: (b, 0, 0))
    bias_rows = jnp.broadcast_to(jnp.tile(bias2, tpos)[:, None], (rows, max(pps * page, LANES)))

    feature_major = lambda cache: jnp.transpose(cache, (0, 1, 3, 4, 2)).reshape(
        cache.shape[0], cache.shape[1], sbw, page)

    def page_spec(n):
        return pl.BlockSpec(
            (1, 1, sbw, page),
            lambda b, c, pt, n=n: (layer, pt[b * n_pages + (nsteps - 1 - c) * pps + n], 0, 0))

    grid_spec = pltpu.PrefetchScalarGridSpec(
        num_scalar_prefetch=1,
        grid=(nb, nsteps),
        in_specs=[per_req(sbw), per_req(sbw), per_req(sbw),
                  pl.BlockSpec(bias_rows.shape, lambda b, c, pt: (0, 0))]
                 + [page_spec(n) for n in range(pps)] * 2,
        out_specs=per_req(sbw),
        scratch_shapes=[pltpu.VMEM((LANES, sbw), F32), pltpu.VMEM((LANES, sbw), F32),
                        pltpu.VMEM((rows, sbw), F32), pltpu.VMEM((rows, SAMPLE_KEY_BLOCK), F32)],
    )
    return pl.pallas_call(
        functools.partial(_attn_sample_kernel, pages_per_step=pps),
        grid_spec=grid_spec,
        out_shape=jax.ShapeDtypeStruct((nb, tpos, sbw), F32),
        compiler_params=pltpu.CompilerParams(dimension_semantics=("arbitrary", "arbitrary"),
                                             vmem_limit_bytes=VMEM_LIMIT_BYTES),
        name="attn_sample",
    )(page_table.reshape(-1), q, k_new, v_new, bias_rows,
      *([feature_major(cache_k)] * pps), *([feature_major(cache_v)] * pps))


def _post_kernel(x_ref, a_ref, gm_ref, gan_ref, ggn_ref, wo_ref, gpost_ref, gpre2_ref,
                 w1_ref, w2_ref, gpost2_ref, y_ref, act_ref):
    hidden = w2_ref.shape[0]
    mix = jnp.concatenate([_rmsnorm(a_ref[...], gan_ref[...]),
                           _rmsnorm(gm_ref[...], ggn_ref[...])], axis=-1).astype(BF16)
    m = jnp.dot(mix, wo_ref[...], preferred_element_type=F32)
    h = x_ref[...] + _rmsnorm(m, gpost_ref[...])
    hn = _rmsnorm(h, gpre2_ref[...]).astype(BF16)
    for j in range(hidden // FFN_COLS):
        lo = j * FFN_COLS
        gate = jnp.dot(hn, w1_ref[:, lo:lo + FFN_COLS], preferred_element_type=F32)
        up = jnp.dot(hn, w1_ref[:, hidden + lo:hidden + lo + FFN_COLS], preferred_element_type=F32)
        act_ref[:, lo:lo + FFN_COLS] = (gate * jax.nn.sigmoid(gate) * up).astype(BF16)
    f = jnp.dot(act_ref[...], w2_ref[...], preferred_element_type=F32)
    y_ref[...] = h + _rmsnorm(f, gpost2_ref[...])


def _post(x, attn_o, gm_o, gan, ggn, wo, gpost, gpre2, w1, w2, gpost2):
    n, d = x.shape
    tm = min(PROJ_ROWS, n)
    hidden = w2.shape[0]
    row_spec = lambda w: pl.BlockSpec((tm, w), lambda i: (i, 0))
    const = lambda a: _resident(a.shape, lambda i: (0,) * a.ndim)
    return pl.pallas_call(
        _post_kernel,
        grid=(n // tm,),
        in_specs=[row_spec(d), row_spec(attn_o.shape[1]), row_spec(gm_o.shape[1]),
                  const(gan), const(ggn), const(wo), const(gpost), const(gpre2),
                  const(w1), const(w2), const(gpost2)],
        out_specs=row_spec(d),
        out_shape=jax.ShapeDtypeStruct((n, d), F32),
        scratch_shapes=[pltpu.VMEM((tm, hidden), BF16)],
        compiler_params=pltpu.CompilerParams(dimension_semantics=("arbitrary",),
                                             vmem_limit_bytes=VMEM_LIMIT_BYTES),
        name="post",
    )(x, attn_o, gm_o, gan, ggn, wo, gpost, gpre2, w1, w2, gpost2)


def kernel(x_prompt, x_sample, cache_k, cache_v, page_table, mix_pre_norm, w_in, sb_bias, gm_v_norm,
           gm_ws, gm_b, attn_out_norm, gm_out_norm, w_o, mix_post_norm, ffn_pre_norm, w_ffn_in,
           w_ffn_out, ffn_post_norm):
    depth = w_in.shape[0]
    b, s, d = x_prompt.shape
    nb, tpos, _ = x_sample.shape
    heads = cache_k.shape[3]
    sbw = heads * cache_k.shape[4]
    gmw = gm_v_norm.shape[1] * gm_v_norm.shape[2]
    groups = gm_ws.shape[1]
    q_scale = LOG2E / float(cache_k.shape[4]) ** 0.5
    row = lambda v: v.reshape(1, -1)

    xp, xs = x_prompt, x_sample.reshape(nb * tpos, d)
    kp_l, vp_l, ksn_l, vsn_l, gvs_l = [], [], [], [], []
    for l in range(depth):
        w = w_in[l]
        w_bf = w.astype(BF16)
        wqvt = jnp.concatenate([w[:, :sbw], w[:, 2 * sbw:3 * sbw]], axis=1).T.astype(BF16)
        ws = gm_ws[l]
        tril = jnp.tril(jnp.ones(ws.shape[1:], ws.dtype))
        ws_tril = (ws * tril).astype(BF16)
        ws_open = (ws * tril)[:, :tpos, :tpos]
        ws_kron = jnp.stack([jnp.kron(jnp.eye(nb, dtype=ws.dtype), ws_open[g]) for g in range(groups)]).astype(BF16)
        bcol = jnp.tile(gm_b[l][:, :tpos].T, (nb, 1))
        bias2 = sb_bias[l] * LOG2E
        gvn = row(gm_v_norm[l])
        post = lambda x, a, g: _post(
            x, a, g, row(attn_out_norm[l]), row(gm_out_norm[l]), w_o[l].astype(BF16), row(mix_post_norm[l]),
            row(ffn_pre_norm[l]), w_ffn_in[l].astype(BF16), w_ffn_out[l].astype(BF16), row(ffn_post_norm[l]))

        k_p, v_p, kb, qt, vt, gm_p = _proj_prompt(
            xp, row(mix_pre_norm[l]), w_bf[:, sbw:], wqvt, gvn, ws_tril, gm_b[l].T, q_scale)
        attn_p = _attn_prompt(bias2, qt, kb, vt)
        xp = post(xp.reshape(b * s, d), attn_p.reshape(b * s, sbw), gm_p.reshape(b * s, gmw)).reshape(b, s, d)
        kp_l.append(k_p.reshape(b, s, heads, -1))
        vp_l.append(v_p.reshape(b, s, heads, -1))

        q_s, k_s, v_s, gvs, gm_s = _proj_sample(xs, row(mix_pre_norm[l]), w_bf, gvn, ws_kron, bcol, q_scale)
        attn_s = _attn_sample(
            l, page_table, q_s.reshape(nb, tpos, sbw), k_s.reshape(nb, tpos, sbw), v_s.reshape(nb, tpos, sbw),
            bias2, cache_k, cache_v)
        xs = post(xs, attn_s.reshape(nb * tpos, sbw), gm_s)
        ksn_l.append(k_s.reshape(nb, tpos, heads, -1))
        vsn_l.append(v_s.reshape(nb, tpos, heads, -1))
        gvs_l.append(gvs.reshape(nb, tpos, gmw))

    stack = lambda per_layer: per_layer[0][None] if depth == 1 else jnp.stack(per_layer)
    return (xp, xs.reshape(nb, tpos, d), stack(kp_l), stack(vp_l), stack(ksn_l), stack(vsn_l), stack(gvs_l))
```

```python
import functools

import jax
import jax.numpy as jnp
from jax import lax
from jax.experimental import pallas as pl
from jax.experimental.pallas import tpu as pltpu

F32 = jnp.float32
BF16 = jnp.bfloat16

RMS_EPS = 1e-6
LOG2E = 1.4426950408889634
HEAD_DIM = 64
GM_GROUP_CH = 128
CHUNK = 128
LANES = 128
ATTN_BLOCK = 256
PROJ_ROWS = 512
FFN_COLS = 256
SAMPLE_KEY_BLOCK = 256
SAMPLE_PAGES_PER_STEP = 16
VMEM_LIMIT_BYTES = 56 * 1024 * 1024
MASK_LOGIT = -1e30

_NT = (((1,), (1,)), ((), ()))


def _log2(n):
    assert n > 0 and n & (n - 1) == 0, n
    return n.bit_length() - 1


def _rmsnorm(x, g):
    r = lax.rsqrt(jnp.mean(x * x, axis=-1, keepdims=True) + RMS_EPS)
    return x * r * g


def _gelu_tanh(x):
    return 0.5 * x * (1.0 + jnp.tanh(0.7978845608028654 * (x + 0.044715 * (x * x * x))))


def _stick_terms(z):
    sign = jnp.uint32(0x80000000)
    neg_abs = lax.bitcast_convert_type(lax.bitcast_convert_type(z, jnp.uint32) | sign, F32)
    e = jnp.exp2(neg_abs)
    l2 = jnp.log(1.0 + e) * LOG2E
    logsig = jnp.minimum(z, 0.0) - l2
    return logsig, logsig - z


def _resident(block_shape, index_map):
    return pl.BlockSpec(block_shape, index_map, pipeline_mode=pl.Buffered(1))


def _proj_prompt_kernel(x_ref, gpre_ref, wnat_ref, wqvt_ref, gvn_ref, ws_ref, bt_ref,
                        k_ref, v_ref, kb_ref, qt_ref, vt_ref, gm_ref, *, q_scale):
    tm = x_ref.shape[1]
    sbw = k_ref.shape[2]
    gmw = gm_ref.shape[2]
    hn = _rmsnorm(x_ref[0], gpre_ref[...]).astype(BF16)
    p = jnp.dot(hn, wnat_ref[...], preferred_element_type=F32)
    k = p[:, :sbw]
    k_ref[0] = k
    kb_ref[0] = k.astype(BF16)
    v_ref[0] = p[:, sbw:2 * sbw]
    pt = lax.dot_general(wqvt_ref[...], hn, _NT, preferred_element_type=F32)
    qt = (pt[:sbw] * q_scale).astype(BF16)
    vt = pt[sbw:].astype(BF16)
    for s in range(tm // ATTN_BLOCK):
        qt_ref[0, s] = qt[:, s * ATTN_BLOCK:(s + 1) * ATTN_BLOCK]
        vt_ref[0, s] = vt[:, s * ATTN_BLOCK:(s + 1) * ATTN_BLOCK]
    gu = _gelu_tanh(p[:, 2 * sbw:2 * sbw + gmw])
    gv = _gelu_tanh(p[:, 2 * sbw + gmw:])
    for g in range(gmw // GM_GROUP_CH):
        cols = slice(g * GM_GROUP_CH, (g + 1) * GM_GROUP_CH)
        gvn = _rmsnorm(gv[:, cols], gvn_ref[:, cols]).astype(BF16)
        for n in range(tm // CHUNK):
            rows = slice(n * CHUNK, (n + 1) * CHUNK)
            s_ = jnp.dot(ws_ref[g], gvn[rows], preferred_element_type=F32) + bt_ref[:, g:g + 1]
            gm_ref[0, rows, cols] = gu[rows, cols] * s_


def _proj_prompt(x, gpre, wnat, wqvt, gvn, ws_tril, bt, q_scale):
    b, s, d = x.shape
    sbw = wqvt.shape[0] // 2
    gmw = gvn.shape[1]
    tm = PROJ_ROWS
    nsub = tm // ATTN_BLOCK
    nblk = s // ATTN_BLOCK
    row_spec = lambda w: pl.BlockSpec((1, tm, w), lambda bi, i: (bi, i, 0))
    t_spec = pl.BlockSpec((1, nsub, sbw, ATTN_BLOCK), lambda bi, i: (bi, i, 0, 0))
    const = lambda shape: pl.BlockSpec(shape, lambda bi, i: (0,) * len(shape))
    return pl.pallas_call(
        functools.partial(_proj_prompt_kernel, q_scale=q_scale),
        grid=(b, s // tm),
        in_specs=[row_spec(d), const(gpre.shape), const(wnat.shape), const(wqvt.shape),
                  const(gvn.shape), const(ws_tril.shape), const(bt.shape)],
        out_specs=[row_spec(sbw), row_spec(sbw), row_spec(sbw), t_spec, t_spec, row_spec(gmw)],
        out_shape=[jax.ShapeDtypeStruct((b, s, sbw), F32), jax.ShapeDtypeStruct((b, s, sbw), F32),
                   jax.ShapeDtypeStruct((b, s, sbw), BF16),
                   jax.ShapeDtypeStruct((b, nblk, sbw, ATTN_BLOCK), BF16),
                   jax.ShapeDtypeStruct((b, nblk, sbw, ATTN_BLOCK), BF16),
                   jax.ShapeDtypeStruct((b, s, gmw), F32)],
        compiler_params=pltpu.CompilerParams(dimension_semantics=("arbitrary", "arbitrary"),
                                             vmem_limit_bytes=VMEM_LIMIT_BYTES),
        name="proj_prompt",
    )(x, gpre, wnat, wqvt, gvn, ws_tril, bt)


def _attn_prompt_kernel(bias_ref, qt_ref, kb_ref, vt_ref, o_ref,
                        wq_ref, zoff_ref, suffix_ref, ls_ref, lb_ref, a_ref, acc_ref, used_ref):
    t = ATTN_BLOCK
    heads = wq_ref.shape[0]
    i = pl.program_id(1)

    @pl.when(i == 0)
    def _():
        row = lax.broadcasted_iota(jnp.int32, (t, t), 0)
        col = lax.broadcasted_iota(jnp.int32, (t, t), 1)
        later = col > row
        suffix_ref[...] = later.astype(BF16)
        for h in range(heads):
            zoff_ref[h, 0] = jnp.where(later, bias_ref[h], bias_ref[h] + MASK_LOGIT)
            zoff_ref[h, 1] = jnp.full((t, t), bias_ref[h], F32)
        ls_ref[...] = jnp.zeros_like(ls_ref)
        lb_ref[...] = jnp.zeros_like(lb_ref)
        a_ref[...] = jnp.zeros_like(a_ref)

    acc_ref[...] = jnp.zeros_like(acc_ref)
    used_ref[...] = jnp.zeros_like(used_ref)
    zeros = jnp.zeros((HEAD_DIM, t), BF16)
    for h in range(heads):
        qh = qt_ref[0, 0, h * HEAD_DIM:(h + 1) * HEAD_DIM, :]
        wq_ref[h] = jnp.concatenate([qh, zeros] if h % 2 == 0 else [zeros, qh], axis=0)

    def step(n, rd, wr):
        ja = jnp.maximum(i - n, 0)
        sel = jnp.minimum(n, 1)
        jc = jnp.clip(i + 2 - n, 0, i)
        live_b = jnp.where(n >= 1, 1.0, 0.0)
        live_c = jnp.where((n >= 2) & (n <= i + 2), 1.0, 0.0)

        def matmuls(h):
            pair = h // 2
            kp = kb_ref[0, pl.ds(pl.multiple_of(ja * t, t), t), pair * LANES:(pair + 1) * LANES]
            return (jnp.dot(kp, wq_ref[h], preferred_element_type=F32),
                    jnp.dot(suffix_ref[...], lb_ref[rd, h], preferred_element_type=F32),
                    jnp.dot(vt_ref[0, jc, h * HEAD_DIM:(h + 1) * HEAD_DIM, :], a_ref[rd, h],
                            preferred_element_type=F32))

        pending = matmuls(0)
        for h in range(heads):
            z, between, p = pending
            if h + 1 < heads:
                pending = matmuls(h + 1)
            logsig, log1m = _stick_terms(z + zoff_ref[h, sel])
            ls_ref[wr, h] = logsig
            lb_ref[wr, h] = log1m.astype(BF16)
            a_ref[wr, h] = jnp.exp2(ls_ref[rd, h] + between).astype(BF16)
            used = used_ref[0, h:h + 1, :]
            block_total = between[0:1, :] + lb_ref[rd, h, 0:1, :].astype(F32)
            used_ref[0, h:h + 1, :] = used + block_total * live_b
            rows = slice(h * HEAD_DIM, (h + 1) * HEAD_DIM)
            acc_ref[rows, :] = acc_ref[rows, :] + p * (jnp.exp2(used_ref[1, h:h + 1, :]) * live_c)
            used_ref[1, h:h + 1, :] = used

    def two_steps(m, carry):
        step(2 * m, 1, 0)
        step(2 * m + 1, 0, 1)
        return carry

    lax.fori_loop(0, (i + 4) // 2, two_steps, 0)
    o_ref[0] = acc_ref[...].T


def _attn_prompt(bias2, qt, kb, vt):
    b, nblk, sbw, t = qt.shape
    s = kb.shape[1]
    heads = sbw // HEAD_DIM
    return pl.pallas_call(
        _attn_prompt_kernel,
        grid=(b, nblk),
        in_specs=[pl.BlockSpec(memory_space=pltpu.SMEM),
                  pl.BlockSpec((1, 1, sbw, t), lambda bi, i: (bi, i, 0, 0)),
                  _resident((1, s, sbw), lambda bi, i: (bi, 0, 0)),
                  _resident((1, nblk, sbw, t), lambda bi, i: (bi, 0, 0, 0))],
        out_specs=pl.BlockSpec((1, t, sbw), lambda bi, i: (bi, i, 0)),
        out_shape=jax.ShapeDtypeStruct((b, s, sbw), F32),
        scratch_shapes=[pltpu.VMEM((heads, 2 * HEAD_DIM, t), BF16),
                        pltpu.VMEM((heads, 2, t, t), F32),
                        pltpu.VMEM((t, t), BF16),
                        pltpu.VMEM((2, heads, t, t), F32),
                        pltpu.VMEM((2, heads, t, t), BF16),
                        pltpu.VMEM((2, heads, t, t), BF16),
                        pltpu.VMEM((sbw, t), F32),
                        pltpu.VMEM((2, heads, t), F32)],
        compiler_params=pltpu.CompilerParams(dimension_semantics=("arbitrary", "arbitrary"),
                                             vmem_limit_bytes=VMEM_LIMIT_BYTES),
        name="attn_prompt",
    )(bias2, qt, kb, vt)


def _proj_sample_kernel(x_ref, gpre_ref, w_ref, gvn_ref, wsk_ref, bcol_ref,
                        q_ref, k_ref, v_ref, gvs_ref, gm_ref, *, q_scale):
    sbw = k_ref.shape[1]
    gmw = gm_ref.shape[1]
    hn = _rmsnorm(x_ref[...], gpre_ref[...]).astype(BF16)
    p = jnp.dot(hn, w_ref[...], preferred_element_type=F32)
    q_ref[...] = (p[:, :sbw] * q_scale).astype(BF16)
    k_ref[...] = p[:, sbw:2 * sbw]
    v_ref[...] = p[:, 2 * sbw:3 * sbw]
    gu = _gelu_tanh(p[:, 3 * sbw:3 * sbw + gmw])
    gv = _gelu_tanh(p[:, 3 * sbw + gmw:])
    for g in range(gmw // GM_GROUP_CH):
        cols = slice(g * GM_GROUP_CH, (g + 1) * GM_GROUP_CH)
        gvn = _rmsnorm(gv[:, cols], gvn_ref[:, cols])
        gvs_ref[:, cols] = gvn
        s_ = jnp.dot(wsk_ref[g], gvn.astype(BF16), preferred_element_type=F32) + bcol_ref[:, g:g + 1]
        gm_ref[:, cols] = gu[:, cols] * s_


def _proj_sample(x, gpre, w, gvn, ws_kron, bcol, q_scale):
    rows, _ = x.shape
    sbw = (w.shape[1] - 2 * gvn.shape[1]) // 3
    gmw = gvn.shape[1]
    vmem = pl.BlockSpec(memory_space=pltpu.VMEM)
    return pl.pallas_call(
        functools.partial(_proj_sample_kernel, q_scale=q_scale),
        in_specs=[vmem] * 6,
        out_specs=[vmem] * 5,
        out_shape=[jax.ShapeDtypeStruct((rows, sbw), BF16), jax.ShapeDtypeStruct((rows, sbw), F32),
                   jax.ShapeDtypeStruct((rows, sbw), F32), jax.ShapeDtypeStruct((rows, gmw), F32),
                   jax.ShapeDtypeStruct((rows, gmw), F32)],
        compiler_params=pltpu.CompilerParams(vmem_limit_bytes=VMEM_LIMIT_BYTES),
        name="proj_sample",
    )(x, gpre, w, gvn, ws_kron, bcol)


def _attn_sample_kernel(pt_ref, q_ref, kn_ref, vn_ref, bias_ref, *rest, pages_per_step):
    del pt_ref
    k_refs = rest[:pages_per_step]
    v_refs = rest[pages_per_step:2 * pages_per_step]
    o_ref, knew_ref, vnew_ref, acc_ref, carry_ref = rest[2 * pages_per_step:]
    tpos, sbw = q_ref.shape[1], q_ref.shape[2]
    heads = sbw // HEAD_DIM
    rows = tpos * heads
    kb = SAMPLE_KEY_BLOCK
    c = pl.program_id(1)

    r_i = lax.broadcasted_iota(jnp.int32, (rows, sbw), 0)
    c_i = lax.broadcasted_iota(jnp.int32, (rows, sbw), 1)
    own_head = (c_i >> _log2(HEAD_DIM)) == (r_i & (heads - 1))
    q = q_ref[0].astype(F32)
    qrep = jnp.concatenate([jnp.broadcast_to(q[t:t + 1, :], (heads, sbw)) for t in range(tpos)], axis=0)
    qbd = jnp.where(own_head, qrep, 0.0).astype(BF16)

    def suffix_and_total(width):
        j = lax.broadcasted_iota(jnp.int32, (width, width), 0)
        s = lax.broadcasted_iota(jnp.int32, (width, width), 1)
        return jnp.concatenate([(j > s).astype(BF16), jnp.ones((width, kb), BF16)], axis=1)

    def sweep(k, v, valid, feature_major):
        width = k.shape[1] if feature_major else k.shape[0]
        blk = min(width, kb)
        if feature_major:
            z = jnp.dot(qbd, k, preferred_element_type=F32)
        else:
            z = lax.dot_general(qbd, k, _NT, preferred_element_type=F32)
        z = z + bias_ref[:, :width]
        logsig, log1m = _stick_terms(z)
        if valid is not None:
            log1m = jnp.where(valid, log1m, 0.0)
        lb = log1m.astype(BF16)
        weights = suffix_and_total(blk)
        used = carry_ref[...]
        between = [None] * (width // blk)
        for b in reversed(range(width // blk)):
            sums = jnp.dot(lb[:, b * blk:(b + 1) * blk], weights, preferred_element_type=F32)
            between[b] = sums[:, :blk] + used[:, :blk]
            used = used + sums[:, blk:]
        a = jnp.exp2(logsig + jnp.concatenate(between, axis=1))
        if valid is not None:
            a = jnp.where(valid, a, 0.0)
        if feature_major:
            acc_ref[...] += lax.dot_general(a.astype(BF16), v, _NT, preferred_element_type=F32)
        else:
            acc_ref[...] += jnp.dot(a.astype(BF16), v, preferred_element_type=F32)
        carry_ref[...] = used

    @pl.when(c == 0)
    def _():
        acc_ref[...] = jnp.zeros_like(acc_ref)
        carry_ref[...] = jnp.zeros_like(carry_ref)
        knew_ref[...] = jnp.zeros_like(knew_ref)
        vnew_ref[...] = jnp.zeros_like(vnew_ref)
        knew_ref[0:tpos, :] = kn_ref[0]
        vnew_ref[0:tpos, :] = vn_ref[0]
        key = lax.broadcasted_iota(jnp.int32, (rows, LANES), 1)
        qpos = lax.broadcasted_iota(jnp.int32, (rows, LANES), 0) >> _log2(heads)
        sweep(knew_ref[...].astype(BF16), vnew_ref[...].astype(BF16), key < qpos, False)

    sweep(jnp.concatenate([r[0, 0] for r in k_refs], axis=1).astype(BF16),
          jnp.concatenate([r[0, 0] for r in v_refs], axis=1).astype(BF16), None, True)

    @pl.when(c == pl.num_programs(1) - 1)
    def _():
        p = jnp.where(own_head, acc_ref[...], 0.0)
        o_ref[0] = jnp.concatenate(
            [jnp.sum(p[t * heads:(t + 1) * heads, :], axis=0, keepdims=True) for t in range(tpos)], axis=0)


def _attn_sample(layer, page_table, q, k_new, v_new, bias2, cache_k, cache_v):
    nb, tpos, sbw = q.shape
    n_pages = page_table.shape[1]
    _, _, page, heads, head_dim = cache_k.shape
    rows = tpos * heads
    pps = SAMPLE_PAGES_PER_STEP if n_pages % SAMPLE_PAGES_PER_STEP == 0 else n_pages
    nsteps = n_pages // pps
    per_req = lambda w: pl.BlockSpec((1, tpos, w), lambda b, c, pt: (b, 0, 0))
    bias_rows = jnp.broadcast_to(jnp.tile(bias2, tpos)[:, None], (rows, max(pps * page, LANES)))

    feature_major = lambda cache: jnp.transpose(cache, (0, 1, 3, 4, 2)).reshape(
        cache.shape[0], cache.shape[1], sbw, page)

    def page_spec(n):
        return pl.BlockSpec(
            (1, 1, sbw, page),
            lambda b, c, pt, n=n: (layer, pt[b * n_pages + (nsteps - 1 - c) * pps + n], 0, 0))

    grid_spec = pltpu.PrefetchScalarGridSpec(
        num_scalar_prefetch=1,
        grid=(nb, nsteps),
        in_specs=[per_req(sbw), per_req(sbw), per_req(sbw),
                  pl.BlockSpec(bias_rows.shape, lambda b, c, pt: (0, 0))]
                 + [page_spec(n) for n in range(pps)] * 2,
        out_specs=per_req(sbw),
        scratch_shapes=[pltpu.VMEM((LANES, sbw), F32), pltpu.VMEM((LANES, sbw), F32),
                        pltpu.VMEM((rows, sbw), F32), pltpu.VMEM((rows, SAMPLE_KEY_BLOCK), F32)],
    )
    return pl.pallas_call(
        functools.partial(_attn_sample_kernel, pages_per_step=pps),
        grid_spec=grid_spec,
        out_shape=jax.ShapeDtypeStruct((nb, tpos, sbw), F32),
        compiler_params=pltpu.CompilerParams(dimension_semantics=("arbitrary", "arbitrary"),
                                             vmem_limit_bytes=VMEM_LIMIT_BYTES),
        name="attn_sample",
    )(page_table.reshape(-1), q, k_new, v_new, bias_rows,
      *([feature_major(cache_k)] * pps), *([feature_major(cache_v)] * pps))


def _post_kernel(x_ref, a_ref, gm_ref, gan_ref, ggn_ref, wo_ref, gpost_ref, gpre2_ref,
                 w1_ref, w2_ref, gpost2_ref, y_ref, act_ref):
    hidden = w2_ref.shape[0]
    mix = jnp.concatenate([_rmsnorm(a_ref[...], gan_ref[...]),
                           _rmsnorm(gm_ref[...], ggn_ref[...])], axis=-1).astype(BF16)
    m = jnp.dot(mix, wo_ref[...], preferred_element_type=F32)
    h = x_ref[...] + _rmsnorm(m, gpost_ref[...])
    hn = _rmsnorm(h, gpre2_ref[...]).astype(BF16)
    for j in range(hidden // FFN_COLS):
        lo = j * FFN_COLS
        gate = jnp.dot(hn, w1_ref[:, lo:lo + FFN_COLS], preferred_element_type=F32)
        up = jnp.dot(hn, w1_ref[:, hidden + lo:hidden + lo + FFN_COLS], preferred_element_type=F32)
        act_ref[:, lo:lo + FFN_COLS] = (gate * jax.nn.sigmoid(gate) * up).astype(BF16)
    f = jnp.dot(act_ref[...], w2_ref[...], preferred_element_type=F32)
    y_ref[...] = h + _rmsnorm(f, gpost2_ref[...])


def _post(x, attn_o, gm_o, gan, ggn, wo, gpost, gpre2, w1, w2, gpost2):
    n, d = x.shape
    tm = min(PROJ_ROWS, n)
    hidden = w2.shape[0]
    row_spec = lambda w: pl.BlockSpec((tm, w), lambda i: (i, 0))
    const = lambda a: _resident(a.shape, lambda i: (0,) * a.ndim)
    return pl.pallas_call(
        _post_kernel,
        grid=(n // tm,),
        in_specs=[row_spec(d), row_spec(attn_o.shape[1]), row_spec(gm_o.shape[1]),
                  const(gan), const(ggn), const(wo), const(gpost), const(gpre2),
                  const(w1), const(w2), const(gpost2)],
        out_specs=row_spec(d),
        out_shape=jax.ShapeDtypeStruct((n, d), F32),
        scratch_shapes=[pltpu.VMEM((tm, hidden), BF16)],
        compiler_params=pltpu.CompilerParams(dimension_semantics=("arbitrary",),
                                             vmem_limit_bytes=VMEM_LIMIT_BYTES),
        name="post",
    )(x, attn_o, gm_o, gan, ggn, wo, gpost, gpre2, w1, w2, gpost2)


def kernel(x_prompt, x_sample, cache_k, cache_v, page_table, mix_pre_norm, w_in, sb_bias, gm_v_norm,
           gm_ws, gm_b, attn_out_norm, gm_out_norm, w_o, mix_post_norm, ffn_pre_norm, w_ffn_in,
           w_ffn_out, ffn_post_norm):
    depth = w_in.shape[0]
    b, s, d = x_prompt.shape
    nb, tpos, _ = x_sample.shape
    heads = cache_k.shape[3]
    sbw = heads * cache_k.shape[4]
    gmw = gm_v_norm.shape[1] * gm_v_norm.shape[2]
    groups = gm_ws.shape[1]
    q_scale = LOG2E / float(cache_k.shape[4]) ** 0.5
    row = lambda v: v.reshape(1, -1)

    xp, xs = x_prompt, x_sample.reshape(nb * tpos, d)
    kp_l, vp_l, ksn_l, vsn_l, gvs_l = [], [], [], [], []
    for l in range(depth):
        w = w_in[l]
        w_bf = w.astype(BF16)
        wqvt = jnp.concatenate([w[:, :sbw], w[:, 2 * sbw:3 * sbw]], axis=1).T.astype(BF16)
        ws = gm_ws[l]
        tril = jnp.tril(jnp.ones(ws.shape[1:], ws.dtype))
        ws_tril = (ws * tril).astype(BF16)
        ws_open = (ws * tril)[:, :tpos, :tpos]
        ws_kron = jnp.stack([jnp.kron(jnp.eye(nb, dtype=ws.dtype), ws_open[g]) for g in range(groups)]).astype(BF16)
        bcol = jnp.tile(gm_b[l][:, :tpos].T, (nb, 1))
        bias2 = sb_bias[l] * LOG2E
        gvn = row(gm_v_norm[l])
        post = lambda x, a, g: _post(
            x, a, g, row(attn_out_norm[l]), row(gm_out_norm[l]), w_o[l].astype(BF16), row(mix_post_norm[l]),
            row(ffn_pre_norm[l]), w_ffn_in[l].astype(BF16), w_ffn_out[l].astype(BF16), row(ffn_post_norm[l]))

        k_p, v_p, kb, qt, vt, gm_p = _proj_prompt(
            xp, row(mix_pre_norm[l]), w_bf[:, sbw:], wqvt, gvn, ws_tril, gm_b[l].T, q_scale)
        attn_p = _attn_prompt(bias2, qt, kb, vt)
        xp = post(xp.reshape(b * s, d), attn_p.reshape(b * s, sbw), gm_p.reshape(b * s, gmw)).reshape(b, s, d)
        kp_l.append(k_p.reshape(b, s, heads, -1))
        vp_l.append(v_p.reshape(b, s, heads, -1))

        q_s, k_s, v_s, gvs, gm_s = _proj_sample(xs, row(mix_pre_norm[l]), w_bf, gvn, ws_kron, bcol, q_scale)
        attn_s = _attn_sample(
            l, page_table, q_s.reshape(nb, tpos, sbw), k_s.reshape(nb, tpos, sbw), v_s.reshape(nb, tpos, sbw),
            bias2, cache_k, cache_v)
        xs = post(xs, attn_s.reshape(nb * tpos, sbw), gm_s)
        ksn_l.append(k_s.reshape(nb, tpos, heads, -1))
        vsn_l.append(v_s.reshape(nb, tpos, heads, -1))
        gvs_l.append(gvs.reshape(nb, tpos, gmw))

    stack = lambda per_layer: per_layer[0][None] if depth == 1 else jnp.stack(per_layer)
    return (xp, xs.reshape(nb, tpos, d), stack(kp_l), stack(vp_l), stack(ksn_l), stack(vsn_l), stack(gvs_l))
```

```python
import functools

import jax
import jax.numpy as jnp
from jax import lax
from jax.experimental import pallas as pl
from jax.experimental.pallas import tpu as pltpu

F32 = jnp.float32
BF16 = jnp.bfloat16

RMS_EPS = 1e-6
LOG2E = 1.4426950408889634
HEAD_DIM = 64
GM_GROUP_CH = 128
CHUNK = 128
LANES = 128
ATTN_BLOCK = 256
PROJ_ROWS = 512
FFN_COLS = 256
SAMPLE_KEY_BLOCK = 256
SAMPLE_PAGES_PER_STEP = 16
VMEM_LIMIT_BYTES = 56 * 1024 * 1024
BIAS_PARTS = 3
MASK_LOGIT = -1e30

_NT = (((1,), (1,)), ((), ()))


def _log2(n):
    assert n > 0 and n & (n - 1) == 0, n
    return n.bit_length() - 1


def _rmsnorm(x, g):
    r = lax.rsqrt(jnp.mean(x * x, axis=-1, keepdims=True) + RMS_EPS)
    return x * r * g


def _gelu_tanh(x):
    return 0.5 * x * (1.0 + jnp.tanh(0.7978845608028654 * (x + 0.044715 * (x * x * x))))


def _stick_terms(z):
    e = jnp.exp2(-jnp.abs(z))
    l2 = jnp.log(1.0 + e) * LOG2E
    logsig = jnp.minimum(z, 0.0) - l2
    return logsig, logsig - z


def _resident(block_shape, index_map):
    return pl.BlockSpec(block_shape, index_map, pipeline_mode=pl.Buffered(1))


def _proj_prompt_kernel(x_ref, gpre_ref, wnat_ref, wqvt_ref, gvn_ref, ws_ref, bt_ref,
                        k_ref, v_ref, kb_ref, qt_ref, vt_ref, gm_ref, *, q_scale):
    tm = x_ref.shape[1]
    sbw = k_ref.shape[2]
    gmw = gm_ref.shape[2]
    hn = _rmsnorm(x_ref[0], gpre_ref[...]).astype(BF16)
    p = jnp.dot(hn, wnat_ref[...], preferred_element_type=F32)
    k = p[:, :sbw]
    k_ref[0] = k
    kb_ref[0] = k.astype(BF16)
    v_ref[0] = p[:, sbw:2 * sbw]
    pt = lax.dot_general(wqvt_ref[...], hn, _NT, preferred_element_type=F32)
    qt = (pt[:sbw] * q_scale).astype(BF16)
    vt = pt[sbw:].astype(BF16)
    for s in range(tm // ATTN_BLOCK):
        qt_ref[0, s] = qt[:, s * ATTN_BLOCK:(s + 1) * ATTN_BLOCK]
        vt_ref[0, s] = vt[:, s * ATTN_BLOCK:(s + 1) * ATTN_BLOCK]
    gu = _gelu_tanh(p[:, 2 * sbw:2 * sbw + gmw])
    gv = _gelu_tanh(p[:, 2 * sbw + gmw:])
    for g in range(gmw // GM_GROUP_CH):
        cols = slice(g * GM_GROUP_CH, (g + 1) * GM_GROUP_CH)
        gvn = _rmsnorm(gv[:, cols], gvn_ref[:, cols]).astype(BF16)
        for n in range(tm // CHUNK):
            rows = slice(n * CHUNK, (n + 1) * CHUNK)
            s_ = jnp.dot(ws_ref[g], gvn[rows], preferred_element_type=F32) + bt_ref[:, g:g + 1]
            gm_ref[0, rows, cols] = gu[rows, cols] * s_


def _proj_prompt(x, gpre, wnat, wqvt, gvn, ws_tril, bt, q_scale):
    b, s, d = x.shape
    sbw = wqvt.shape[0] // 2
    gmw = gvn.shape[1]
    tm = PROJ_ROWS
    nsub = tm // ATTN_BLOCK
    nblk = s // ATTN_BLOCK
    row_spec = lambda w: pl.BlockSpec((1, tm, w), lambda bi, i: (bi, i, 0))
    t_spec = pl.BlockSpec((1, nsub, sbw, ATTN_BLOCK), lambda bi, i: (bi, i, 0, 0))
    const = lambda shape: pl.BlockSpec(shape, lambda bi, i: (0,) * len(shape))
    return pl.pallas_call(
        functools.partial(_proj_prompt_kernel, q_scale=q_scale),
        grid=(b, s // tm),
        in_specs=[row_spec(d), const(gpre.shape), const(wnat.shape), const(wqvt.shape),
                  const(gvn.shape), const(ws_tril.shape), const(bt.shape)],
        out_specs=[row_spec(sbw), row_spec(sbw), row_spec(sbw), t_spec, t_spec, row_spec(gmw)],
        out_shape=[jax.ShapeDtypeStruct((b, s, sbw), F32), jax.ShapeDtypeStruct((b, s, sbw), F32),
                   jax.ShapeDtypeStruct((b, s, sbw), BF16),
                   jax.ShapeDtypeStruct((b, nblk, sbw, ATTN_BLOCK), BF16),
                   jax.ShapeDtypeStruct((b, nblk, sbw, ATTN_BLOCK), BF16),
                   jax.ShapeDtypeStruct((b, s, gmw), F32)],
        compiler_params=pltpu.CompilerParams(dimension_semantics=("arbitrary", "arbitrary"),
                                             vmem_limit_bytes=VMEM_LIMIT_BYTES),
        name="proj_prompt",
    )(x, gpre, wnat, wqvt, gvn, ws_tril, bt)


def _attn_prompt_kernel(bias_ref, qt_ref, kb_ref, vt_ref, o_ref,
                        wq_ref, ones_ref, suffix_ref, ls_ref, lb_ref, a_ref, acc_ref, used_ref):
    t = ATTN_BLOCK
    heads = wq_ref.shape[0]
    i = pl.program_id(1)
    row = lax.broadcasted_iota(jnp.int32, (t, t), 0)
    col = lax.broadcasted_iota(jnp.int32, (t, t), 1)
    later = col > row

    @pl.when(i == 0)
    def _():
        suffix_ref[...] = later.astype(BF16)
        lane = lax.broadcasted_iota(jnp.int32, (t, LANES), 1)
        ones_ref[...] = (lane < BIAS_PARTS).astype(BF16)
        ls_ref[...] = jnp.zeros_like(ls_ref)
        lb_ref[...] = jnp.zeros_like(lb_ref)
        a_ref[...] = jnp.zeros_like(a_ref)
        part_row = lax.broadcasted_iota(jnp.int32, (LANES, t), 0)
        for h in range(heads):
            rest = jnp.full((LANES, t), bias_ref[h], F32)
            rows = jnp.zeros((LANES, t), F32)
            for part in range(BIAS_PARTS):
                piece = rest.astype(BF16).astype(F32)
                rows = jnp.where(part_row == part, piece, rows)
                rest = rest - piece
            wq_ref[h, LANES:, :] = rows.astype(BF16)

    acc_ref[...] = jnp.zeros_like(acc_ref)
    used_ref[...] = jnp.zeros_like(used_ref)
    zeros = jnp.zeros((HEAD_DIM, t), BF16)
    for h in range(heads):
        qh = qt_ref[0, 0, h * HEAD_DIM:(h + 1) * HEAD_DIM, :]
        wq_ref[h, :LANES, :] = jnp.concatenate([qh, zeros] if h % 2 == 0 else [zeros, qh], axis=0)

    def logits(j, h):
        pair = h // 2
        kp = kb_ref[0, pl.ds(pl.multiple_of(j * t, t), t), pair * LANES:(pair + 1) * LANES]
        keys = jnp.concatenate([kp, ones_ref[...]], axis=1)
        return jnp.dot(keys, wq_ref[h], preferred_element_type=F32)

    for h in range(heads):
        logsig, log1m = _stick_terms(jnp.where(later, logits(i, h), MASK_LOGIT))
        ls_ref[0, h] = logsig
        lb_ref[0, h] = log1m.astype(BF16)

    def step(n, rd, wr):
        ja = jnp.maximum(i - n, 0)
        jc = jnp.clip(i + 2 - n, 0, i)
        live_c = jnp.where((n >= 2) & (n <= i + 2), 1.0, 0.0)

        def matmuls(h):
            return (logits(ja, h),
                    jnp.dot(suffix_ref[...], lb_ref[rd, h], preferred_element_type=F32),
                    jnp.dot(vt_ref[0, jc, h * HEAD_DIM:(h + 1) * HEAD_DIM, :], a_ref[rd, h],
                            preferred_element_type=F32))

        pending = matmuls(0)
        for h in range(heads):
            z, between, p = pending
            if h + 1 < heads:
                pending = matmuls(h + 1)
            logsig, log1m = _stick_terms(z)
            ls_ref[wr, h] = logsig
            lb_ref[wr, h] = log1m.astype(BF16)
            a_ref[wr, h] = jnp.exp2(ls_ref[rd, h] + between).astype(BF16)
            used = used_ref[0, h:h + 1, :]
            used_ref[0, h:h + 1, :] = used + between[0:1, :] + lb_ref[rd, h, 0:1, :].astype(F32)
            rows = slice(h * HEAD_DIM, (h + 1) * HEAD_DIM)
            acc_ref[rows, :] = acc_ref[rows, :] + p * (jnp.exp2(used_ref[1, h:h + 1, :]) * live_c)
            used_ref[1, h:h + 1, :] = used

    def two_steps(m, carry):
        step(2 * m + 1, 0, 1)
        step(2 * m + 2, 1, 0)
        return carry

    lax.fori_loop(0, (i + 3) // 2, two_steps, 0)
    o_ref[0] = acc_ref[...].T


def _attn_prompt(bias2, qt, kb, vt):
    b, nblk, sbw, t = qt.shape
    s = kb.shape[1]
    heads = sbw // HEAD_DIM
    return pl.pallas_call(
        _attn_prompt_kernel,
        grid=(b, nblk),
        in_specs=[pl.BlockSpec(memory_space=pltpu.SMEM),
                  pl.BlockSpec((1, 1, sbw, t), lambda bi, i: (bi, i, 0, 0)),
                  _resident((1, s, sbw), lambda bi, i: (bi, 0, 0)),
                  _resident((1, nblk, sbw, t), lambda bi, i: (bi, 0, 0, 0))],
        out_specs=pl.BlockSpec((1, t, sbw), lambda bi, i: (bi, i, 0)),
        out_shape=jax.ShapeDtypeStruct((b, s, sbw), F32),
        scratch_shapes=[pltpu.VMEM((heads, 2 * LANES, t), BF16),
                        pltpu.VMEM((t, LANES), BF16),
                        pltpu.VMEM((t, t), BF16),
                        pltpu.VMEM((2, heads, t, t), F32),
                        pltpu.VMEM((2, heads, t, t), BF16),
                        pltpu.VMEM((2, heads, t, t), BF16),
                        pltpu.VMEM((sbw, t), F32),
                        pltpu.VMEM((2, heads, t), F32)],
        compiler_params=pltpu.CompilerParams(dimension_semantics=("arbitrary", "arbitrary"),
                                             vmem_limit_bytes=VMEM_LIMIT_BYTES),
        name="attn_prompt",
    )(bias2, qt, kb, vt)


def _proj_sample_kernel(x_ref, gpre_ref, w_ref, gvn_ref, wsk_ref, bcol_ref,
                        q_ref, k_ref, v_ref, gvs_ref, gm_ref, *, q_scale):
    sbw = k_ref.shape[1]
    gmw = gm_ref.shape[1]
    hn = _rmsnorm(x_ref[...], gpre_ref[...]).astype(BF16)
    p = jnp.dot(hn, w_ref[...], preferred_element_type=F32)
    q_ref[...] = (p[:, :sbw] * q_scale).astype(BF16)
    k_ref[...] = p[:, sbw:2 * sbw]
    v_ref[...] = p[:, 2 * sbw:3 * sbw]
    gu = _gelu_tanh(p[:, 3 * sbw:3 * sbw + gmw])
    gv = _gelu_tanh(p[:, 3 * sbw + gmw:])
    for g in range(gmw // GM_GROUP_CH):
        cols = slice(g * GM_GROUP_CH, (g + 1) * GM_GROUP_CH)
        gvn = _rmsnorm(gv[:, cols], gvn_ref[:, cols])
        gvs_ref[:, cols] = gvn
        s_ = jnp.dot(wsk_ref[g], gvn.astype(BF16), preferred_element_type=F32) + bcol_ref[:, g:g + 1]
        gm_ref[:, cols] = gu[:, cols] * s_


def _proj_sample(x, gpre, w, gvn, ws_kron, bcol, q_scale):
    rows, _ = x.shape
    sbw = (w.shape[1] - 2 * gvn.shape[1]) // 3
    gmw = gvn.shape[1]
    vmem = pl.BlockSpec(memory_space=pltpu.VMEM)
    return pl.pallas_call(
        functools.partial(_proj_sample_kernel, q_scale=q_scale),
        in_specs=[vmem] * 6,
        out_specs=[vmem] * 5,
        out_shape=[jax.ShapeDtypeStruct((rows, sbw), BF16), jax.ShapeDtypeStruct((rows, sbw), F32),
                   jax.ShapeDtypeStruct((rows, sbw), F32), jax.ShapeDtypeStruct((rows, gmw), F32),
                   jax.ShapeDtypeStruct((rows, gmw), F32)],
        compiler_params=pltpu.CompilerParams(vmem_limit_bytes=VMEM_LIMIT_BYTES),
        name="proj_sample",
    )(x, gpre, w, gvn, ws_kron, bcol)


def _attn_sample_kernel(pt_ref, q_ref, kn_ref, vn_ref, bias_ref, *rest, pages_per_step):
    del pt_ref
    k_refs = rest[:pages_per_step]
    v_refs = rest[pages_per_step:2 * pages_per_step]
    o_ref, knew_ref, vnew_ref, acc_ref, carry_ref = rest[2 * pages_per_step:]
    tpos, sbw = q_ref.shape[1], q_ref.shape[2]
    heads = sbw // HEAD_DIM
    rows = tpos * heads
    kb = SAMPLE_KEY_BLOCK
    c = pl.program_id(1)

    r_i = lax.broadcasted_iota(jnp.int32, (rows, sbw), 0)
    c_i = lax.broadcasted_iota(jnp.int32, (rows, sbw), 1)
    own_head = (c_i >> _log2(HEAD_DIM)) == (r_i & (heads - 1))
    q = q_ref[0].astype(F32)
    qrep = jnp.concatenate([jnp.broadcast_to(q[t:t + 1, :], (heads, sbw)) for t in range(tpos)], axis=0)
    qbd = jnp.where(own_head, qrep, 0.0).astype(BF16)

    def suffix_and_total(width):
        j = lax.broadcasted_iota(jnp.int32, (width, width), 0)
        s = lax.broadcasted_iota(jnp.int32, (width, width), 1)
        return jnp.concatenate([(j > s).astype(BF16), jnp.ones((width, kb), BF16)], axis=1)

    def sweep(k, v, valid, feature_major):
        width = k.shape[1] if feature_major else k.shape[0]
        blk = min(width, kb)
        if feature_major:
            z = jnp.dot(qbd, k, preferred_element_type=F32)
        else:
            z = lax.dot_general(qbd, k, _NT, preferred_element_type=F32)
        z = z + bias_ref[:, :width]
        logsig, log1m = _stick_terms(z)
        if valid is not None:
            log1m = jnp.where(valid, log1m, 0.0)
        lb = log1m.astype(BF16)
        weights = suffix_and_total(blk)
        used = carry_ref[...]
        between = [None] * (width // blk)
        for b in reversed(range(width // blk)):
            sums = jnp.dot(lb[:, b * blk:(b + 1) * blk], weights, preferred_element_type=F32)
            between[b] = sums[:, :blk] + used[:, :blk]
            used = used + sums[:, blk:]
        a = jnp.exp2(logsig + jnp.concatenate(between, axis=1))
        if valid is not None:
            a = jnp.where(valid, a, 0.0)
        if feature_major:
            acc_ref[...] += lax.dot_general(a.astype(BF16), v, _NT, preferred_element_type=F32)
        else:
            acc_ref[...] += jnp.dot(a.astype(BF16), v, preferred_element_type=F32)
        carry_ref[...] = used

    @pl.when(c == 0)
    def _():
        acc_ref[...] = jnp.zeros_like(acc_ref)
        carry_ref[...] = jnp.zeros_like(carry_ref)
        knew_ref[...] = jnp.zeros_like(knew_ref)
        vnew_ref[...] = jnp.zeros_like(vnew_ref)
        knew_ref[0:tpos, :] = kn_ref[0]
        vnew_ref[0:tpos, :] = vn_ref[0]
        key = lax.broadcasted_iota(jnp.int32, (rows, LANES), 1)
        qpos = lax.broadcasted_iota(jnp.int32, (rows, LANES), 0) >> _log2(heads)
        sweep(knew_ref[...].astype(BF16), vnew_ref[...].astype(BF16), key < qpos, False)

    sweep(jnp.concatenate([r[0, 0] for r in k_refs], axis=1).astype(BF16),
          jnp.concatenate([r[0, 0] for r in v_refs], axis=1).astype(BF16), None, True)

    @pl.when(c == pl.num_programs(1) - 1)
    def _():
        p = jnp.where(own_head, acc_ref[...], 0.0)
        o_ref[0] = jnp.concatenate(
            [jnp.sum(p[t * heads:(t + 1) * heads, :], axis=0, keepdims=True) for t in range(tpos)], axis=0)


def _attn_sample(layer, page_table, q, k_new, v_new, bias2, cache_k, cache_v):
    nb, tpos, sbw = q.shape
    n_pages = page_table.shape[1]
    _, _, page, heads, head_dim = cache_k.shape
    rows = tpos * heads
    pps = SAMPLE_PAGES_PER_STEP if n_pages % SAMPLE_PAGES_PER_STEP == 0 else n_pages
    nsteps = n_pages // pps
    per_req = lambda w: pl.BlockSpec((1, tpos, w), lambda b, c, pt: (b, 0, 0))
    bias_rows = jnp.broadcast_to(jnp.tile(bias2, tpos)[:, None], (rows, max(pps * page, LANES)))

    feature_major = lambda cache: jnp.transpose(cache, (0, 1, 3, 4, 2)).reshape(
        cache.shape[0], cache.shape[1], sbw, page)

    def page_spec(n):
        return pl.BlockSpec(
            (1, 1, sbw, page),
            lambda b, c, pt, n=n: (layer, pt[b * n_pages + (nsteps - 1 - c) * pps + n], 0, 0))

    grid_spec = pltpu.PrefetchScalarGridSpec(
        num_scalar_prefetch=1,
        grid=(nb, nsteps),
        in_specs=[per_req(sbw), per_req(sbw), per_req(sbw),
                  pl.BlockSpec(bias_rows.shape, lambda b, c, pt: (0, 0))]
                 + [page_spec(n) for n in range(pps)] * 2,
        out_specs=per_req(sbw),
        scratch_shapes=[pltpu.VMEM((LANES, sbw), F32), pltpu.VMEM((LANES, sbw), F32),
                        pltpu.VMEM((rows, sbw), F32), pltpu.VMEM((rows, SAMPLE_KEY_BLOCK), F32)],
    )
    return pl.pallas_call(
        functools.partial(_attn_sample_kernel, pages_per_step=pps),
        grid_spec=grid_spec,
        out_shape=jax.ShapeDtypeStruct((nb, tpos, sbw), F32),
        compiler_params=pltpu.CompilerParams(dimension_semantics=("arbitrary", "arbitrary"),
                                             vmem_limit_bytes=VMEM_LIMIT_BYTES),
        name="attn_sample",
    )(page_table.reshape(-1), q, k_new, v_new, bias_rows,
      *([feature_major(cache_k)] * pps), *([feature_major(cache_v)] * pps))


def _post_kernel(x_ref, a_ref, gm_ref, gan_ref, ggn_ref, wo_ref, gpost_ref, gpre2_ref,
                 w1_ref, w2_ref, gpost2_ref, y_ref, act_ref):
    hidden = w2_ref.shape[0]
    mix = jnp.concatenate([_rmsnorm(a_ref[...], gan_ref[...]),
                           _rmsnorm(gm_ref[...], ggn_ref[...])], axis=-1).astype(BF16)
    m = jnp.dot(mix, wo_ref[...], preferred_element_type=F32)
    h = x_ref[...] + _rmsnorm(m, gpost_ref[...])
    hn = _rmsnorm(h, gpre2_ref[...]).astype(BF16)
    for j in range(hidden // FFN_COLS):
        lo = j * FFN_COLS
        gate = jnp.dot(hn, w1_ref[:, lo:lo + FFN_COLS], preferred_element_type=F32)
        up = jnp.dot(hn, w1_ref[:, hidden + lo:hidden + lo + FFN_COLS], preferred_element_type=F32)
        act_ref[:, lo:lo + FFN_COLS] = (gate * jax.nn.sigmoid(gate) * up).astype(BF16)
    f = jnp.dot(act_ref[...], w2_ref[...], preferred_element_type=F32)
    y_ref[...] = h + _rmsnorm(f, gpost2_ref[...])


def _post(x, attn_o, gm_o, gan, ggn, wo, gpost, gpre2, w1, w2, gpost2):
    n, d = x.shape
    tm = min(PROJ_ROWS, n)
    hidden = w2.shape[0]
    row_spec = lambda w: pl.BlockSpec((tm, w), lambda i: (i, 0))
    const = lambda a: _resident(a.shape, lambda i: (0,) * a.ndim)
    return pl.pallas_call(
        _post_kernel,
        grid=(n // tm,),
        in_specs=[row_spec(d), row_spec(attn_o.shape[1]), row_spec(gm_o.shape[1]),
                  const(gan), const(ggn), const(wo), const(gpost), const(gpre2),
                  const(w1), const(w2), const(gpost2)],
        out_specs=row_spec(d),
        out_shape=jax.ShapeDtypeStruct((n, d), F32),
        scratch_shapes=[pltpu.VMEM((tm, hidden), BF16)],
        compiler_params=pltpu.CompilerParams(dimension_semantics=("arbitrary",),
                                             vmem_limit_bytes=VMEM_LIMIT_BYTES),
        name="post",
    )(x, attn_o, gm_o, gan, ggn, wo, gpost, gpre2, w1, w2, gpost2)


def kernel(x_prompt, x_sample, cache_k, cache_v, page_table, mix_pre_norm, w_in, sb_bias, gm_v_norm,
           gm_ws, gm_b, attn_out_norm, gm_out_norm, w_o, mix_post_norm, ffn_pre_norm, w_ffn_in,
           w_ffn_out, ffn_post_norm):
    depth = w_in.shape[0]
    b, s, d = x_prompt.shape
    nb, tpos, _ = x_sample.shape
    heads = cache_k.shape[3]
    sbw = heads * cache_k.shape[4]
    gmw = gm_v_norm.shape[1] * gm_v_norm.shape[2]
    groups = gm_ws.shape[1]
    q_scale = LOG2E / float(cache_k.shape[4]) ** 0.5
    row = lambda v: v.reshape(1, -1)

    xp, xs = x_prompt, x_sample.reshape(nb * tpos, d)
    kp_l, vp_l, ksn_l, vsn_l, gvs_l = [], [], [], [], []
    for l in range(depth):
        w = w_in[l]
        w_bf = w.astype(BF16)
        wqvt = jnp.concatenate([w[:, :sbw], w[:, 2 * sbw:3 * sbw]], axis=1).T.astype(BF16)
        ws = gm_ws[l]
        tril = jnp.tril(jnp.ones(ws.shape[1:], ws.dtype))
        ws_tril = (ws * tril).astype(BF16)
        ws_open = (ws * tril)[:, :tpos, :tpos]
        ws_kron = jnp.stack([jnp.kron(jnp.eye(nb, dtype=ws.dtype), ws_open[g]) for g in range(groups)]).astype(BF16)
        bcol = jnp.tile(gm_b[l][:, :tpos].T, (nb, 1))
        bias2 = sb_bias[l] * LOG2E
        gvn = row(gm_v_norm[l])
        post = lambda x, a, g: _post(
            x, a, g, row(attn_out_norm[l]), row(gm_out_norm[l]), w_o[l].astype(BF16), row(mix_post_norm[l]),
            row(ffn_pre_norm[l]), w_ffn_in[l].astype(BF16), w_ffn_out[l].astype(BF16), row(ffn_post_norm[l]))

        k_p, v_p, kb, qt, vt, gm_p = _proj_prompt(
            xp, row(mix_pre_norm[l]), w_bf[:, sbw:], wqvt, gvn, ws_tril, gm_b[l].T, q_scale)
        attn_p = _attn_prompt(bias2, qt, kb, vt)
        xp = post(xp.reshape(b * s, d), attn_p.reshape(b * s, sbw), gm_p.reshape(b * s, gmw)).reshape(b, s, d)
        kp_l.append(k_p.reshape(b, s, heads, -1))
        vp_l.append(v_p.reshape(b, s, heads, -1))

        q_s, k_s, v_s, gvs, gm_s = _proj_sample(xs, row(mix_pre_norm[l]), w_bf, gvn, ws_kron, bcol, q_scale)
        attn_s = _attn_sample(
            l, page_table, q_s.reshape(nb, tpos, sbw), k_s.reshape(nb, tpos, sbw), v_s.reshape(nb, tpos, sbw),
            bias2, cache_k, cache_v)
        xs = post(xs, attn_s.reshape(nb * tpos, sbw), gm_s)
        ksn_l.append(k_s.reshape(nb, tpos, heads, -1))
        vsn_l.append(v_s.reshape(nb, tpos, heads, -1))
        gvs_l.append(gvs.reshape(nb, tpos, gmw))

    stack = lambda per_layer: per_layer[0][None] if depth == 1 else jnp.stack(per_layer)
    return (xp, xs.reshape(nb, tpos, d), stack(kp_l), stack(vp_l), stack(ksn_l), stack(vsn_l), stack(gvs_l))
```

```python
import functools

import jax
import jax.numpy as jnp
from jax import lax
from jax.experimental import pallas as pl
from jax.experimental.pallas import tpu as pltpu

F32 = jnp.float32
BF16 = jnp.bfloat16

RMS_EPS = 1e-6
LOG2E = 1.4426950408889634
HEAD_DIM = 64
GM_GROUP_CH = 128
CHUNK = 128
LANES = 128
ATTN_BLOCK = 256
PROJ_ROWS = 512
FFN_COLS = 256
SAMPLE_KEY_BLOCK = 256
SAMPLE_PAGES_PER_STEP = 32
VMEM_LIMIT_BYTES = 56 * 1024 * 1024
BIAS_PARTS = 3
MASK_LOGIT = -1e30

_NT = (((1,), (1,)), ((), ()))


def _log2(n):
    assert n > 0 and n & (n - 1) == 0, n
    return n.bit_length() - 1


def _rmsnorm(x, g):
    r = lax.rsqrt(jnp.mean(x * x, axis=-1, keepdims=True) + RMS_EPS)
    return x * r * g


def _gelu_tanh(x):
    return 0.5 * x * (1.0 + jnp.tanh(0.7978845608028654 * (x + 0.044715 * (x * x * x))))


def _stick_terms(z):
    e = jnp.exp2(-jnp.abs(z))
    l2 = jnp.log(1.0 + e) * LOG2E
    logsig = jnp.minimum(z, 0.0) - l2
    return logsig, logsig - z


def _resident(block_shape, index_map):
    return pl.BlockSpec(block_shape, index_map, pipeline_mode=pl.Buffered(1))


def _proj_prompt_kernel(x_ref, gpre_ref, wnat_ref, wqvt_ref, gvn_ref, ws_ref, bt_ref,
                        k_ref, v_ref, kb_ref, qt_ref, vt_ref, gm_ref, *, q_scale):
    tm = x_ref.shape[1]
    sbw = k_ref.shape[2]
    gmw = gm_ref.shape[2]
    hn = _rmsnorm(x_ref[0], gpre_ref[...]).astype(BF16)
    p = jnp.dot(hn, wnat_ref[...], preferred_element_type=F32)
    k = p[:, :sbw]
    k_ref[0] = k
    kb_ref[0] = k.astype(BF16)
    v_ref[0] = p[:, sbw:2 * sbw]
    pt = lax.dot_general(wqvt_ref[...], hn, _NT, preferred_element_type=F32)
    qt = (pt[:sbw] * q_scale).astype(BF16)
    vt = pt[sbw:].astype(BF16)
    for s in range(tm // ATTN_BLOCK):
        qt_ref[0, s] = qt[:, s * ATTN_BLOCK:(s + 1) * ATTN_BLOCK]
        vt_ref[0, s] = vt[:, s * ATTN_BLOCK:(s + 1) * ATTN_BLOCK]
    gu = _gelu_tanh(p[:, 2 * sbw:2 * sbw + gmw])
    gv = _gelu_tanh(p[:, 2 * sbw + gmw:])
    for g in range(gmw // GM_GROUP_CH):
        cols = slice(g * GM_GROUP_CH, (g + 1) * GM_GROUP_CH)
        gvn = _rmsnorm(gv[:, cols], gvn_ref[:, cols]).astype(BF16)
        for n in range(tm // CHUNK):
            rows = slice(n * CHUNK, (n + 1) * CHUNK)
            s_ = jnp.dot(ws_ref[g], gvn[rows], preferred_element_type=F32) + bt_ref[:, g:g + 1]
            gm_ref[0, rows, cols] = gu[rows, cols] * s_


def _proj_prompt(x, gpre, wnat, wqvt, gvn, ws_tril, bt, q_scale):
    b, s, d = x.shape
    sbw = wqvt.shape[0] // 2
    gmw = gvn.shape[1]
    tm = PROJ_ROWS
    nsub = tm // ATTN_BLOCK
    nblk = s // ATTN_BLOCK
    row_spec = lambda w: pl.BlockSpec((1, tm, w), lambda bi, i: (bi, i, 0))
    t_spec = pl.BlockSpec((1, nsub, sbw, ATTN_BLOCK), lambda bi, i: (bi, i, 0, 0))
    const = lambda shape: pl.BlockSpec(shape, lambda bi, i: (0,) * len(shape))
    return pl.pallas_call(
        functools.partial(_proj_prompt_kernel, q_scale=q_scale),
        grid=(b, s // tm),
        in_specs=[row_spec(d), const(gpre.shape), const(wnat.shape), const(wqvt.shape),
                  const(gvn.shape), const(ws_tril.shape), const(bt.shape)],
        out_specs=[row_spec(sbw), row_spec(sbw), row_spec(sbw), t_spec, t_spec, row_spec(gmw)],
        out_shape=[jax.ShapeDtypeStruct((b, s, sbw), F32), jax.ShapeDtypeStruct((b, s, sbw), F32),
                   jax.ShapeDtypeStruct((b, s, sbw), BF16),
                   jax.ShapeDtypeStruct((b, nblk, sbw, ATTN_BLOCK), BF16),
                   jax.ShapeDtypeStruct((b, nblk, sbw, ATTN_BLOCK), BF16),
                   jax.ShapeDtypeStruct((b, s, gmw), F32)],
        compiler_params=pltpu.CompilerParams(dimension_semantics=("arbitrary", "arbitrary"),
                                             vmem_limit_bytes=VMEM_LIMIT_BYTES),
        name="proj_prompt",
    )(x, gpre, wnat, wqvt, gvn, ws_tril, bt)


def _attn_prompt_kernel(bias_ref, qt_ref, kb_ref, vt_ref, o_ref,
                        wq_ref, ones_ref, suffix_ref, ls_ref, lb_ref, a_ref, acc_ref, used_ref):
    t = ATTN_BLOCK
    heads = wq_ref.shape[0]
    i = pl.program_id(1)
    row = lax.broadcasted_iota(jnp.int32, (t, t), 0)
    col = lax.broadcasted_iota(jnp.int32, (t, t), 1)
    later = col > row

    @pl.when(i == 0)
    def _():
        suffix_ref[...] = later.astype(BF16)
        lane = lax.broadcasted_iota(jnp.int32, (t, LANES), 1)
        ones_ref[...] = (lane < BIAS_PARTS).astype(BF16)
        ls_ref[...] = jnp.zeros_like(ls_ref)
        lb_ref[...] = jnp.zeros_like(lb_ref)
        a_ref[...] = jnp.zeros_like(a_ref)
        part_row = lax.broadcasted_iota(jnp.int32, (LANES, t), 0)
        for h in range(heads):
            rest = jnp.full((LANES, t), bias_ref[h], F32)
            rows = jnp.zeros((LANES, t), F32)
            for part in range(BIAS_PARTS):
                piece = rest.astype(BF16).astype(F32)
                rows = jnp.where(part_row == part, piece, rows)
                rest = rest - piece
            wq_ref[h, LANES:, :] = rows.astype(BF16)

    acc_ref[...] = jnp.zeros_like(acc_ref)
    used_ref[...] = jnp.zeros_like(used_ref)
    zeros = jnp.zeros((HEAD_DIM, t), BF16)
    for h in range(heads):
        qh = qt_ref[0, 0, h * HEAD_DIM:(h + 1) * HEAD_DIM, :]
        wq_ref[h, :LANES, :] = jnp.concatenate([qh, zeros] if h % 2 == 0 else [zeros, qh], axis=0)

    def logits(j, h):
        pair = h // 2
        kp = kb_ref[0, pl.ds(pl.multiple_of(j * t, t), t), pair * LANES:(pair + 1) * LANES]
        keys = jnp.concatenate([kp, ones_ref[...]], axis=1)
        return jnp.dot(keys, wq_ref[h], preferred_element_type=F32)

    for h in range(heads):
        logsig, log1m = _stick_terms(jnp.where(later, logits(i, h), MASK_LOGIT))
        ls_ref[0, h] = logsig
        lb_ref[0, h] = log1m.astype(BF16)

    def step(n, rd, wr):
        ja = jnp.maximum(i - n, 0)
        jc = jnp.clip(i + 2 - n, 0, i)
        live_c = jnp.where((n >= 2) & (n <= i + 2), 1.0, 0.0)

        def matmuls(h):
            return (logits(ja, h),
                    jnp.dot(suffix_ref[...], lb_ref[rd, h], preferred_element_type=F32),
                    jnp.dot(vt_ref[0, jc, h * HEAD_DIM:(h + 1) * HEAD_DIM, :], a_ref[rd, h],
                            preferred_element_type=F32))

        pending = matmuls(0)
        for h in range(heads):
            z, between, p = pending
            if h + 1 < heads:
                pending = matmuls(h + 1)
            logsig, log1m = _stick_terms(z)
            ls_ref[wr, h] = logsig
            lb_ref[wr, h] = log1m.astype(BF16)
            a_ref[wr, h] = jnp.exp2(ls_ref[rd, h] + between).astype(BF16)
            used = used_ref[0, h:h + 1, :]
            used_ref[0, h:h + 1, :] = used + between[0:1, :] + lb_ref[rd, h, 0:1, :].astype(F32)
            rows = slice(h * HEAD_DIM, (h + 1) * HEAD_DIM)
            acc_ref[rows, :] = acc_ref[rows, :] + p * (jnp.exp2(used_ref[1, h:h + 1, :]) * live_c)
            used_ref[1, h:h + 1, :] = used

    def two_steps(m, carry):
        step(2 * m + 1, 0, 1)
        step(2 * m + 2, 1, 0)
        return carry

    lax.fori_loop(0, (i + 3) // 2, two_steps, 0)
    o_ref[0] = acc_ref[...].T


def _attn_prompt(bias2, qt, kb, vt):
    b, nblk, sbw, t = qt.shape
    s = kb.shape[1]
    heads = sbw // HEAD_DIM
    return pl.pallas_call(
        _attn_prompt_kernel,
        grid=(b, nblk),
        in_specs=[pl.BlockSpec(memory_space=pltpu.SMEM),
                  pl.BlockSpec((1, 1, sbw, t), lambda bi, i: (bi, i, 0, 0)),
                  _resident((1, s, sbw), lambda bi, i: (bi, 0, 0)),
                  _resident((1, nblk, sbw, t), lambda bi, i: (bi, 0, 0, 0))],
        out_specs=pl.BlockSpec((1, t, sbw), lambda bi, i: (bi, i, 0)),
        out_shape=jax.ShapeDtypeStruct((b, s, sbw), F32),
        scratch_shapes=[pltpu.VMEM((heads, 2 * LANES, t), BF16),
                        pltpu.VMEM((t, LANES), BF16),
                        pltpu.VMEM((t, t), BF16),
                        pltpu.VMEM((2, heads, t, t), F32),
                        pltpu.VMEM((2, heads, t, t), BF16),
                        pltpu.VMEM((2, heads, t, t), BF16),
                        pltpu.VMEM((sbw, t), F32),
                        pltpu.VMEM((2, heads, t), F32)],
        compiler_params=pltpu.CompilerParams(dimension_semantics=("arbitrary", "arbitrary"),
                                             vmem_limit_bytes=VMEM_LIMIT_BYTES),
        name="attn_prompt",
    )(bias2, qt, kb, vt)


def _proj_sample_kernel(x_ref, gpre_ref, w_ref, gvn_ref, wsk_ref, bcol_ref,
                        q_ref, k_ref, v_ref, gvs_ref, gm_ref, *, q_scale):
    sbw = k_ref.shape[1]
    gmw = gm_ref.shape[1]
    hn = _rmsnorm(x_ref[...], gpre_ref[...]).astype(BF16)
    p = jnp.dot(hn, w_ref[...], preferred_element_type=F32)
    q_ref[...] = (p[:, :sbw] * q_scale).astype(BF16)
    k_ref[...] = p[:, sbw:2 * sbw]
    v_ref[...] = p[:, 2 * sbw:3 * sbw]
    gu = _gelu_tanh(p[:, 3 * sbw:3 * sbw + gmw])
    gv = _gelu_tanh(p[:, 3 * sbw + gmw:])
    for g in range(gmw // GM_GROUP_CH):
        cols = slice(g * GM_GROUP_CH, (g + 1) * GM_GROUP_CH)
        gvn = _rmsnorm(gv[:, cols], gvn_ref[:, cols])
        gvs_ref[:, cols] = gvn
        s_ = jnp.dot(wsk_ref[g], gvn.astype(BF16), preferred_element_type=F32) + bcol_ref[:, g:g + 1]
        gm_ref[:, cols] = gu[:, cols] * s_


def _proj_sample(x, gpre, w, gvn, ws_kron, bcol, q_scale):
    rows, _ = x.shape
    sbw = (w.shape[1] - 2 * gvn.shape[1]) // 3
    gmw = gvn.shape[1]
    vmem = pl.BlockSpec(memory_space=pltpu.VMEM)
    return pl.pallas_call(
        functools.partial(_proj_sample_kernel, q_scale=q_scale),
        in_specs=[vmem] * 6,
        out_specs=[vmem] * 5,
        out_shape=[jax.ShapeDtypeStruct((rows, sbw), BF16), jax.ShapeDtypeStruct((rows, sbw), F32),
                   jax.ShapeDtypeStruct((rows, sbw), F32), jax.ShapeDtypeStruct((rows, gmw), F32),
                   jax.ShapeDtypeStruct((rows, gmw), F32)],
        compiler_params=pltpu.CompilerParams(vmem_limit_bytes=VMEM_LIMIT_BYTES),
        name="proj_sample",
    )(x, gpre, w, gvn, ws_kron, bcol)


def _attn_sample_kernel(pt_ref, q_ref, kn_ref, vn_ref, bias_ref, *rest, pages_per_step):
    del pt_ref
    k_refs = rest[:pages_per_step]
    v_refs = rest[pages_per_step:2 * pages_per_step]
    o_ref, knew_ref, vnew_ref, acc_ref, carry_ref = rest[2 * pages_per_step:]
    tpos, sbw = q_ref.shape[1], q_ref.shape[2]
    heads = sbw // HEAD_DIM
    rows = tpos * heads
    kb = SAMPLE_KEY_BLOCK
    c = pl.program_id(1)

    r_i = lax.broadcasted_iota(jnp.int32, (rows, sbw), 0)
    c_i = lax.broadcasted_iota(jnp.int32, (rows, sbw), 1)
    own_head = (c_i >> _log2(HEAD_DIM)) == (r_i & (heads - 1))
    q = q_ref[0].astype(F32)
    qrep = jnp.concatenate([jnp.broadcast_to(q[t:t + 1, :], (heads, sbw)) for t in range(tpos)], axis=0)
    qbd = jnp.where(own_head, qrep, 0.0).astype(BF16)

    def suffix_and_total(width):
        j = lax.broadcasted_iota(jnp.int32, (width, width), 0)
        s = lax.broadcasted_iota(jnp.int32, (width, width), 1)
        return jnp.concatenate([(j > s).astype(BF16), jnp.ones((width, kb), BF16)], axis=1)

    def sweep(k, v, valid, feature_major):
        width = k.shape[1] if feature_major else k.shape[0]
        blk = min(width, kb)
        if feature_major:
            z = jnp.dot(qbd, k, preferred_element_type=F32)
        else:
            z = lax.dot_general(qbd, k, _NT, preferred_element_type=F32)
        z = z + bias_ref[:, :width]
        logsig, log1m = _stick_terms(z)
        if valid is not None:
            log1m = jnp.where(valid, log1m, 0.0)
        lb = log1m.astype(BF16)
        weights = suffix_and_total(blk)
        used = carry_ref[...]
        nblk = width // blk
        stacked = jnp.concatenate([lb[:, b * blk:(b + 1) * blk] for b in range(nblk)], axis=0)
        sums = jnp.dot(stacked, weights, preferred_element_type=F32)
        between = [None] * nblk
        for b in reversed(range(nblk)):
            block_sums = sums[b * rows:(b + 1) * rows]
            between[b] = block_sums[:, :blk] + used[:, :blk]
            used = used + block_sums[:, blk:]
        a = jnp.exp2(logsig + jnp.concatenate(between, axis=1))
        if valid is not None:
            a = jnp.where(valid, a, 0.0)
        if feature_major:
            acc_ref[...] += lax.dot_general(a.astype(BF16), v, _NT, preferred_element_type=F32)
        else:
            acc_ref[...] += jnp.dot(a.astype(BF16), v, preferred_element_type=F32)
        carry_ref[...] = used

    @pl.when(c == 0)
    def _():
        acc_ref[...] = jnp.zeros_like(acc_ref)
        carry_ref[...] = jnp.zeros_like(carry_ref)
        knew_ref[...] = jnp.zeros_like(knew_ref)
        vnew_ref[...] = jnp.zeros_like(vnew_ref)
        knew_ref[0:tpos, :] = kn_ref[0]
        vnew_ref[0:tpos, :] = vn_ref[0]
        key = lax.broadcasted_iota(jnp.int32, (rows, LANES), 1)
        qpos = lax.broadcasted_iota(jnp.int32, (rows, LANES), 0) >> _log2(heads)
        sweep(knew_ref[...].astype(BF16), vnew_ref[...].astype(BF16), key < qpos, False)

    sweep(jnp.concatenate([r[0, 0] for r in k_refs], axis=1).astype(BF16),
          jnp.concatenate([r[0, 0] for r in v_refs], axis=1).astype(BF16), None, True)

    @pl.when(c == pl.num_programs(1) - 1)
    def _():
        p = jnp.where(own_head, acc_ref[...], 0.0)
        o_ref[0] = jnp.concatenate(
            [jnp.sum(p[t * heads:(t + 1) * heads, :], axis=0, keepdims=True) for t in range(tpos)], axis=0)


def _attn_sample(layer, page_table, q, k_new, v_new, bias2, cache_k, cache_v):
    nb, tpos, sbw = q.shape
    n_pages = page_table.shape[1]
    _, _, page, heads, head_dim = cache_k.shape
    rows = tpos * heads
    pps = SAMPLE_PAGES_PER_STEP if n_pages % SAMPLE_PAGES_PER_STEP == 0 else n_pages
    nsteps = n_pages // pps
    per_req = lambda w: pl.BlockSpec((1, tpos, w), lambda b, c, pt: (b, 0, 0))
    bias_rows = jnp.broadcast_to(jnp.tile(bias2, tpos)[:, None], (rows, max(pps * page, LANES)))

    feature_major = lambda cache: jnp.transpose(cache, (0, 1, 3, 4, 2)).reshape(
        cache.shape[0], cache.shape[1], sbw, page)

    def page_spec(n):
        return pl.BlockSpec(
            (1, 1, sbw, page),
            lambda b, c, pt, n=n: (layer, pt[b * n_pages + (nsteps - 1 - c) * pps + n], 0, 0))

    grid_spec = pltpu.PrefetchScalarGridSpec(
        num_scalar_prefetch=1,
        grid=(nb, nsteps),
        in_specs=[per_req(sbw), per_req(sbw), per_req(sbw),
                  pl.BlockSpec(bias_rows.shape, lambda b, c, pt: (0, 0))]
                 + [page_spec(n) for n in range(pps)] * 2,
        out_specs=per_req(sbw),
        scratch_shapes=[pltpu.VMEM((LANES, sbw), F32), pltpu.VMEM((LANES, sbw), F32),
                        pltpu.VMEM((rows, sbw), F32), pltpu.VMEM((rows, SAMPLE_KEY_BLOCK), F32)],
    )
    return pl.pallas_call(
        functools.partial(_attn_sample_kernel, pages_per_step=pps),
        grid_spec=grid_spec,
        out_shape=jax.ShapeDtypeStruct((nb, tpos, sbw), F32),
        compiler_params=pltpu.CompilerParams(dimension_semantics=("arbitrary", "arbitrary"),
                                             vmem_limit_bytes=VMEM_LIMIT_BYTES),
        name="attn_sample",
    )(page_table.reshape(-1), q, k_new, v_new, bias_rows,
      *([feature_major(cache_k)] * pps), *([feature_major(cache_v)] * pps))


def _post_kernel(x_ref, a_ref, gm_ref, gan_ref, ggn_ref, wo_ref, gpost_ref, gpre2_ref,
                 w1_ref, w2_ref, gpost2_ref, y_ref, act_ref):
    hidden = w2_ref.shape[0]
    mix = jnp.concatenate([_rmsnorm(a_ref[...], gan_ref[...]),
                           _rmsnorm(gm_ref[...], ggn_ref[...])], axis=-1).astype(BF16)
    m = jnp.dot(mix, wo_ref[...], preferred_element_type=F32)
    h = x_ref[...] + _rmsnorm(m, gpost_ref[...])
    hn = _rmsnorm(h, gpre2_ref[...]).astype(BF16)
    for j in range(hidden // FFN_COLS):
        lo = j * FFN_COLS
        gate = jnp.dot(hn, w1_ref[:, lo:lo + FFN_COLS], preferred_element_type=F32)
        up = jnp.dot(hn, w1_ref[:, hidden + lo:hidden + lo + FFN_COLS], preferred_element_type=F32)
        act_ref[:, lo:lo + FFN_COLS] = (gate * jax.nn.sigmoid(gate) * up).astype(BF16)
    f = jnp.dot(act_ref[...], w2_ref[...], preferred_element_type=F32)
    y_ref[...] = h + _rmsnorm(f, gpost2_ref[...])


def _post(x, attn_o, gm_o, gan, ggn, wo, gpost, gpre2, w1, w2, gpost2):
    n, d = x.shape
    tm = min(PROJ_ROWS, n)
    hidden = w2.shape[0]
    row_spec = lambda w: pl.BlockSpec((tm, w), lambda i: (i, 0))
    const = lambda a: _resident(a.shape, lambda i: (0,) * a.ndim)
    return pl.pallas_call(
        _post_kernel,
        grid=(n // tm,),
        in_specs=[row_spec(d), row_spec(attn_o.shape[1]), row_spec(gm_o.shape[1]),
                  const(gan), const(ggn), const(wo), const(gpost), const(gpre2),
                  const(w1), const(w2), const(gpost2)],
        out_specs=row_spec(d),
        out_shape=jax.ShapeDtypeStruct((n, d), F32),
        scratch_shapes=[pltpu.VMEM((tm, hidden), BF16)],
        compiler_params=pltpu.CompilerParams(dimension_semantics=("arbitrary",),
                                             vmem_limit_bytes=VMEM_LIMIT_BYTES),
        name="post",
    )(x, attn_o, gm_o, gan, ggn, wo, gpost, gpre2, w1, w2, gpost2)


def kernel(x_prompt, x_sample, cache_k, cache_v, page_table, mix_pre_norm, w_in, sb_bias, gm_v_norm,
           gm_ws, gm_b, attn_out_norm, gm_out_norm, w_o, mix_post_norm, ffn_pre_norm, w_ffn_in,
           w_ffn_out, ffn_post_norm):
    depth = w_in.shape[0]
    b, s, d = x_prompt.shape
    nb, tpos, _ = x_sample.shape
    heads = cache_k.shape[3]
    sbw = heads * cache_k.shape[4]
    gmw = gm_v_norm.shape[1] * gm_v_norm.shape[2]
    groups = gm_ws.shape[1]
    q_scale = LOG2E / float(cache_k.shape[4]) ** 0.5
    row = lambda v: v.reshape(1, -1)

    xp, xs = x_prompt, x_sample.reshape(nb * tpos, d)
    kp_l, vp_l, ksn_l, vsn_l, gvs_l = [], [], [], [], []
    for l in range(depth):
        w = w_in[l]
        w_bf = w.astype(BF16)
        wqvt = jnp.concatenate([w[:, :sbw], w[:, 2 * sbw:3 * sbw]], axis=1).T.astype(BF16)
        ws = gm_ws[l]
        tril = jnp.tril(jnp.ones(ws.shape[1:], ws.dtype))
        ws_tril = (ws * tril).astype(BF16)
        ws_open = (ws * tril)[:, :tpos, :tpos]
        ws_kron = jnp.stack([jnp.kron(jnp.eye(nb, dtype=ws.dtype), ws_open[g]) for g in range(groups)]).astype(BF16)
        bcol = jnp.tile(gm_b[l][:, :tpos].T, (nb, 1))
        bias2 = sb_bias[l] * LOG2E
        gvn = row(gm_v_norm[l])
        post = lambda x, a, g: _post(
            x, a, g, row(attn_out_norm[l]), row(gm_out_norm[l]), w_o[l].astype(BF16), row(mix_post_norm[l]),
            row(ffn_pre_norm[l]), w_ffn_in[l].astype(BF16), w_ffn_out[l].astype(BF16), row(ffn_post_norm[l]))

        k_p, v_p, kb, qt, vt, gm_p = _proj_prompt(
            xp, row(mix_pre_norm[l]), w_bf[:, sbw:], wqvt, gvn, ws_tril, gm_b[l].T, q_scale)
        attn_p = _attn_prompt(bias2, qt, kb, vt)
        xp = post(xp.reshape(b * s, d), attn_p.reshape(b * s, sbw), gm_p.reshape(b * s, gmw)).reshape(b, s, d)
        kp_l.append(k_p.reshape(b, s, heads, -1))
        vp_l.append(v_p.reshape(b, s, heads, -1))

        q_s, k_s, v_s, gvs, gm_s = _proj_sample(xs, row(mix_pre_norm[l]), w_bf, gvn, ws_kron, bcol, q_scale)
        attn_s = _attn_sample(
            l, page_table, q_s.reshape(nb, tpos, sbw), k_s.reshape(nb, tpos, sbw), v_s.reshape(nb, tpos, sbw),
            bias2, cache_k, cache_v)
        xs = post(xs, attn_s.reshape(nb * tpos, sbw), gm_s)
        ksn_l.append(k_s.reshape(nb, tpos, heads, -1))
        vsn_l.append(v_s.reshape(nb, tpos, heads, -1))
        gvs_l.append(gvs.reshape(nb, tpos, gmw))

    stack = lambda per_layer: per_layer[0][None] if depth == 1 else jnp.stack(per_layer)
    return (xp, xs.reshape(nb, tpos, d), stack(kp_l), stack(vp_l), stack(ksn_l), stack(vsn_l), stack(gvs_l))
```

```python
import functools

import jax
import jax.numpy as jnp
from jax import lax
from jax.experimental import pallas as pl
from jax.experimental.pallas import tpu as pltpu

F32 = jnp.float32
BF16 = jnp.bfloat16

RMS_EPS = 1e-6
LOG2E = 1.4426950408889634
HEAD_DIM = 64
GM_GROUP_CH = 128
CHUNK = 128
LANES = 128
ATTN_BLOCK = 256
PROJ_ROWS = 512
FFN_COLS = 256
SAMPLE_KEY_BLOCK = 256
SAMPLE_PAGES_PER_STEP = 32
VMEM_LIMIT_BYTES = 56 * 1024 * 1024
BIAS_PARTS = 3
MASK_LOGIT = -1e30

_NT = (((1,), (1,)), ((), ()))


def _log2(n):
    assert n > 0 and n & (n - 1) == 0, n
    return n.bit_length() - 1


def _rmsnorm(x, g):
    r = lax.rsqrt(jnp.mean(x * x, axis=-1, keepdims=True) + RMS_EPS)
    return x * r * g


def _gelu_tanh(x):
    return 0.5 * x * (1.0 + jnp.tanh(0.7978845608028654 * (x + 0.044715 * (x * x * x))))


def _stick_terms(z):
    e = jnp.exp2(-jnp.abs(z))
    l2 = jnp.log(1.0 + e) * LOG2E
    logsig = jnp.minimum(z, 0.0) - l2
    return logsig, logsig - z


def _resident(block_shape, index_map):
    return pl.BlockSpec(block_shape, index_map, pipeline_mode=pl.Buffered(1))


def _proj_prompt_kernel(x_ref, gpre_ref, wnat_ref, wqvt_ref, gvn_ref, ws_ref, bt_ref,
                        k_ref, v_ref, kb_ref, qt_ref, vt_ref, gm_ref, *, q_scale):
    tm = x_ref.shape[1]
    sbw = kb_ref.shape[2]
    gmw = gm_ref.shape[2]
    hn = _rmsnorm(x_ref[0], gpre_ref[...]).astype(BF16)
    p = jnp.dot(hn, wnat_ref[...], preferred_element_type=F32)
    kb_ref[0] = p[:, :sbw].astype(BF16)
    pt = lax.dot_general(wqvt_ref[...], hn, _NT, preferred_element_type=F32)
    qt = (pt[:sbw] * q_scale).astype(BF16)
    k_ref[0] = pt[sbw:2 * sbw]
    v_ref[0] = pt[2 * sbw:]
    vt = pt[2 * sbw:].astype(BF16)
    for s in range(tm // ATTN_BLOCK):
        qt_ref[0, s] = qt[:, s * ATTN_BLOCK:(s + 1) * ATTN_BLOCK]
        vt_ref[0, s] = vt[:, s * ATTN_BLOCK:(s + 1) * ATTN_BLOCK]
    gu = _gelu_tanh(p[:, 2 * sbw:2 * sbw + gmw])
    gv = _gelu_tanh(p[:, 2 * sbw + gmw:])
    for g in range(gmw // GM_GROUP_CH):
        cols = slice(g * GM_GROUP_CH, (g + 1) * GM_GROUP_CH)
        gvn = _rmsnorm(gv[:, cols], gvn_ref[:, cols]).astype(BF16)
        for n in range(tm // CHUNK):
            rows = slice(n * CHUNK, (n + 1) * CHUNK)
            s_ = jnp.dot(ws_ref[g], gvn[rows], preferred_element_type=F32) + bt_ref[:, g:g + 1]
            gm_ref[0, rows, cols] = gu[rows, cols] * s_


def _proj_prompt(x, gpre, wnat, wqvt, gvn, ws_tril, bt, q_scale):
    b, s, d = x.shape
    sbw = wqvt.shape[0] // 3
    gmw = gvn.shape[1]
    tm = PROJ_ROWS
    ft_spec = pl.BlockSpec((1, sbw, tm), lambda bi, i: (bi, 0, i))
    nsub = tm // ATTN_BLOCK
    nblk = s // ATTN_BLOCK
    row_spec = lambda w: pl.BlockSpec((1, tm, w), lambda bi, i: (bi, i, 0))
    t_spec = pl.BlockSpec((1, nsub, sbw, ATTN_BLOCK), lambda bi, i: (bi, i, 0, 0))
    const = lambda shape: pl.BlockSpec(shape, lambda bi, i: (0,) * len(shape))
    return pl.pallas_call(
        functools.partial(_proj_prompt_kernel, q_scale=q_scale),
        grid=(b, s // tm),
        in_specs=[row_spec(d), const(gpre.shape), const(wnat.shape), const(wqvt.shape),
                  const(gvn.shape), const(ws_tril.shape), const(bt.shape)],
        out_specs=[ft_spec, ft_spec, row_spec(sbw), t_spec, t_spec, row_spec(gmw)],
        out_shape=[jax.ShapeDtypeStruct((b, sbw, s), F32), jax.ShapeDtypeStruct((b, sbw, s), F32),
                   jax.ShapeDtypeStruct((b, s, sbw), BF16),
                   jax.ShapeDtypeStruct((b, nblk, sbw, ATTN_BLOCK), BF16),
                   jax.ShapeDtypeStruct((b, nblk, sbw, ATTN_BLOCK), BF16),
                   jax.ShapeDtypeStruct((b, s, gmw), F32)],
        compiler_params=pltpu.CompilerParams(dimension_semantics=("arbitrary", "arbitrary"),
                                             vmem_limit_bytes=VMEM_LIMIT_BYTES),
        name="proj_prompt",
    )(x, gpre, wnat, wqvt, gvn, ws_tril, bt)


def _attn_prompt_kernel(bias_ref, qt_ref, kb_ref, vt_ref, o_ref,
                        wq_ref, ones_ref, suffix_ref, ls_ref, lb_ref, a_ref, acc_ref, used_ref):
    t = ATTN_BLOCK
    heads = wq_ref.shape[0]
    i = pl.program_id(1)
    row = lax.broadcasted_iota(jnp.int32, (t, t), 0)
    col = lax.broadcasted_iota(jnp.int32, (t, t), 1)
    later = col > row

    @pl.when(i == 0)
    def _():
        suffix_ref[...] = later.astype(BF16)
        lane = lax.broadcasted_iota(jnp.int32, (t, LANES), 1)
        ones_ref[...] = (lane < BIAS_PARTS).astype(BF16)
        ls_ref[...] = jnp.zeros_like(ls_ref)
        lb_ref[...] = jnp.zeros_like(lb_ref)
        a_ref[...] = jnp.zeros_like(a_ref)
        part_row = lax.broadcasted_iota(jnp.int32, (LANES, t), 0)
        for h in range(heads):
            rest = jnp.full((LANES, t), bias_ref[h], F32)
            rows = jnp.zeros((LANES, t), F32)
            for part in range(BIAS_PARTS):
                piece = rest.astype(BF16).astype(F32)
                rows = jnp.where(part_row == part, piece, rows)
                rest = rest - piece
            wq_ref[h, LANES:, :] = rows.astype(BF16)

    acc_ref[...] = jnp.zeros_like(acc_ref)
    used_ref[...] = jnp.zeros_like(used_ref)
    zeros = jnp.zeros((HEAD_DIM, t), BF16)
    for h in range(heads):
        qh = qt_ref[0, 0, h * HEAD_DIM:(h + 1) * HEAD_DIM, :]
        wq_ref[h, :LANES, :] = jnp.concatenate([qh, zeros] if h % 2 == 0 else [zeros, qh], axis=0)

    def logits(j, h):
        pair = h // 2
        kp = kb_ref[0, pl.ds(pl.multiple_of(j * t, t), t), pair * LANES:(pair + 1) * LANES]
        keys = jnp.concatenate([kp, ones_ref[...]], axis=1)
        return jnp.dot(keys, wq_ref[h], preferred_element_type=F32)

    for h in range(heads):
        logsig, log1m = _stick_terms(jnp.where(later, logits(i, h), MASK_LOGIT))
        ls_ref[0, h] = logsig
        lb_ref[0, h] = log1m.astype(BF16)

    def step(n, rd, wr):
        ja = jnp.maximum(i - n, 0)
        jc = jnp.clip(i + 2 - n, 0, i)
        live_c = jnp.where((n >= 2) & (n <= i + 2), 1.0, 0.0)

        def matmuls(h):
            return (logits(ja, h),
                    jnp.dot(suffix_ref[...], lb_ref[rd, h], preferred_element_type=F32),
                    jnp.dot(vt_ref[0, jc, h * HEAD_DIM:(h + 1) * HEAD_DIM, :], a_ref[rd, h],
                            preferred_element_type=F32))

        pending = matmuls(0)
        for h in range(heads):
            z, between, p = pending
            if h + 1 < heads:
                pending = matmuls(h + 1)
            logsig, log1m = _stick_terms(z)
            ls_ref[wr, h] = logsig
            lb_ref[wr, h] = log1m.astype(BF16)
            a_ref[wr, h] = jnp.exp2(ls_ref[rd, h] + between).astype(BF16)
            used = used_ref[0, h:h + 1, :]
            used_ref[0, h:h + 1, :] = used + between[0:1, :] + lb_ref[rd, h, 0:1, :].astype(F32)
            rows = slice(h * HEAD_DIM, (h + 1) * HEAD_DIM)
            acc_ref[rows, :] = acc_ref[rows, :] + p * (jnp.exp2(used_ref[1, h:h + 1, :]) * live_c)
            used_ref[1, h:h + 1, :] = used

    def two_steps(m, carry):
        step(2 * m + 1, 0, 1)
        step(2 * m + 2, 1, 0)
        return carry

    lax.fori_loop(0, (i + 3) // 2, two_steps, 0)
    o_ref[0] = acc_ref[...].T


def _attn_prompt(bias2, qt, kb, vt):
    b, nblk, sbw, t = qt.shape
    s = kb.shape[1]
    heads = sbw // HEAD_DIM
    return pl.pallas_call(
        _attn_prompt_kernel,
        grid=(b, nblk),
        in_specs=[pl.BlockSpec(memory_space=pltpu.SMEM),
                  pl.BlockSpec((1, 1, sbw, t), lambda bi, i: (bi, i, 0, 0)),
                  _resident((1, s, sbw), lambda bi, i: (bi, 0, 0)),
                  _resident((1, nblk, sbw, t), lambda bi, i: (bi, 0, 0, 0))],
        out_specs=pl.BlockSpec((1, t, sbw), lambda bi, i: (bi, i, 0)),
        out_shape=jax.ShapeDtypeStruct((b, s, sbw), F32),
        scratch_shapes=[pltpu.VMEM((heads, 2 * LANES, t), BF16),
                        pltpu.VMEM((t, LANES), BF16),
                        pltpu.VMEM((t, t), BF16),
                        pltpu.VMEM((2, heads, t, t), F32),
                        pltpu.VMEM((2, heads, t, t), BF16),
                        pltpu.VMEM((2, heads, t, t), BF16),
                        pltpu.VMEM((sbw, t), F32),
                        pltpu.VMEM((2, heads, t), F32)],
        compiler_params=pltpu.CompilerParams(dimension_semantics=("arbitrary", "arbitrary"),
                                             vmem_limit_bytes=VMEM_LIMIT_BYTES),
        name="attn_prompt",
    )(bias2, qt, kb, vt)


def _proj_sample_kernel(x_ref, gpre_ref, w_ref, gvn_ref, wsk_ref, bcol_ref,
                        q_ref, k_ref, v_ref, gvs_ref, gm_ref, *, q_scale):
    sbw = k_ref.shape[1]
    gmw = gm_ref.shape[1]
    hn = _rmsnorm(x_ref[...], gpre_ref[...]).astype(BF16)
    p = jnp.dot(hn, w_ref[...], preferred_element_type=F32)
    q_ref[...] = (p[:, :sbw] * q_scale).astype(BF16)
    k_ref[...] = p[:, sbw:2 * sbw]
    v_ref[...] = p[:, 2 * sbw:3 * sbw]
    gu = _gelu_tanh(p[:, 3 * sbw:3 * sbw + gmw])
    gv = _gelu_tanh(p[:, 3 * sbw + gmw:])
    for g in range(gmw // GM_GROUP_CH):
        cols = slice(g * GM_GROUP_CH, (g + 1) * GM_GROUP_CH)
        gvn = _rmsnorm(gv[:, cols], gvn_ref[:, cols])
        gvs_ref[:, cols] = gvn
        s_ = jnp.dot(wsk_ref[g], gvn.astype(BF16), preferred_element_type=F32) + bcol_ref[:, g:g + 1]
        gm_ref[:, cols] = gu[:, cols] * s_


def _proj_sample(x, gpre, w, gvn, ws_kron, bcol, q_scale):
    rows, _ = x.shape
    sbw = (w.shape[1] - 2 * gvn.shape[1]) // 3
    gmw = gvn.shape[1]
    vmem = pl.BlockSpec(memory_space=pltpu.VMEM)
    return pl.pallas_call(
        functools.partial(_proj_sample_kernel, q_scale=q_scale),
        in_specs=[vmem] * 6,
        out_specs=[vmem] * 5,
        out_shape=[jax.ShapeDtypeStruct((rows, sbw), BF16), jax.ShapeDtypeStruct((rows, sbw), F32),
                   jax.ShapeDtypeStruct((rows, sbw), F32), jax.ShapeDtypeStruct((rows, gmw), F32),
                   jax.ShapeDtypeStruct((rows, gmw), F32)],
        compiler_params=pltpu.CompilerParams(vmem_limit_bytes=VMEM_LIMIT_BYTES),
        name="proj_sample",
    )(x, gpre, w, gvn, ws_kron, bcol)


def _attn_sample_kernel(pt_ref, q_ref, kn_ref, vn_ref, bias_ref, *rest, pages_per_step):
    del pt_ref
    k_refs = rest[:pages_per_step]
    v_refs = rest[pages_per_step:2 * pages_per_step]
    o_ref, knew_ref, vnew_ref, acc_ref, carry_ref = rest[2 * pages_per_step:]
    tpos, sbw = q_ref.shape[1], q_ref.shape[2]
    heads = sbw // HEAD_DIM
    rows = tpos * heads
    kb = SAMPLE_KEY_BLOCK
    c = pl.program_id(1)

    r_i = lax.broadcasted_iota(jnp.int32, (rows, sbw), 0)
    c_i = lax.broadcasted_iota(jnp.int32, (rows, sbw), 1)
    own_head = (c_i >> _log2(HEAD_DIM)) == (r_i & (heads - 1))
    q = q_ref[0].astype(F32)
    qrep = jnp.concatenate([jnp.broadcast_to(q[t:t + 1, :], (heads, sbw)) for t in range(tpos)], axis=0)
    qbd = jnp.where(own_head, qrep, 0.0).astype(BF16)

    def suffix_and_total(width):
        j = lax.broadcasted_iota(jnp.int32, (width, width), 0)
        s = lax.broadcasted_iota(jnp.int32, (width, width), 1)
        return jnp.concatenate([(j > s).astype(BF16), jnp.ones((width, kb), BF16)], axis=1)

    def sweep(k, v, valid, feature_major):
        width = k.shape[1] if feature_major else k.shape[0]
        blk = min(width, kb)
        if feature_major:
            z = jnp.dot(qbd, k, preferred_element_type=F32)
        else:
            z = lax.dot_general(qbd, k, _NT, preferred_element_type=F32)
        z = z + bias_ref[:, :width]
        logsig, log1m = _stick_terms(z)
        if valid is not None:
            log1m = jnp.where(valid, log1m, 0.0)
        lb = log1m.astype(BF16)
        weights = suffix_and_total(blk)
        used = carry_ref[...]
        nblk = width // blk
        stacked = jnp.concatenate([lb[:, b * blk:(b + 1) * blk] for b in range(nblk)], axis=0)
        sums = jnp.dot(stacked, weights, preferred_element_type=F32)
        between = [None] * nblk
        for b in reversed(range(nblk)):
            block_sums = sums[b * rows:(b + 1) * rows]
            between[b] = block_sums[:, :blk] + used[:, :blk]
            used = used + block_sums[:, blk:]
        a = jnp.exp2(logsig + jnp.concatenate(between, axis=1))
        if valid is not None:
            a = jnp.where(valid, a, 0.0)
        if feature_major:
            acc_ref[...] += lax.dot_general(a.astype(BF16), v, _NT, preferred_element_type=F32)
        else:
            acc_ref[...] += jnp.dot(a.astype(BF16), v, preferred_element_type=F32)
        carry_ref[...] = used

    @pl.when(c == 0)
    def _():
        acc_ref[...] = jnp.zeros_like(acc_ref)
        carry_ref[...] = jnp.zeros_like(carry_ref)
        knew_ref[...] = jnp.zeros_like(knew_ref)
        vnew_ref[...] = jnp.zeros_like(vnew_ref)
        knew_ref[0:tpos, :] = kn_ref[0]
        vnew_ref[0:tpos, :] = vn_ref[0]
        key = lax.broadcasted_iota(jnp.int32, (rows, LANES), 1)
        qpos = lax.broadcasted_iota(jnp.int32, (rows, LANES), 0) >> _log2(heads)
        sweep(knew_ref[...].astype(BF16), vnew_ref[...].astype(BF16), key < qpos, False)

    sweep(jnp.concatenate([r[0, 0] for r in k_refs], axis=1).astype(BF16),
          jnp.concatenate([r[0, 0] for r in v_refs], axis=1).astype(BF16), None, True)

    @pl.when(c == pl.num_programs(1) - 1)
    def _():
        p = jnp.where(own_head, acc_ref[...], 0.0)
        o_ref[0] = jnp.concatenate(
            [jnp.sum(p[t * heads:(t + 1) * heads, :], axis=0, keepdims=True) for t in range(tpos)], axis=0)


def _attn_sample(layer, page_table, q, k_new, v_new, bias2, cache_k, cache_v):
    nb, tpos, sbw = q.shape
    n_pages = page_table.shape[1]
    _, _, page, heads, head_dim = cache_k.shape
    rows = tpos * heads
    pps = SAMPLE_PAGES_PER_STEP if n_pages % SAMPLE_PAGES_PER_STEP == 0 else n_pages
    nsteps = n_pages // pps
    per_req = lambda w: pl.BlockSpec((1, tpos, w), lambda b, c, pt: (b, 0, 0))
    bias_rows = jnp.broadcast_to(jnp.tile(bias2, tpos)[:, None], (rows, max(pps * page, LANES)))

    feature_major = lambda cache: jnp.transpose(cache, (0, 1, 3, 4, 2)).reshape(
        cache.shape[0], cache.shape[1], sbw, page)

    def page_spec(n):
        return pl.BlockSpec(
            (1, 1, sbw, page),
            lambda b, c, pt, n=n: (layer, pt[b * n_pages + (nsteps - 1 - c) * pps + n], 0, 0))

    grid_spec = pltpu.PrefetchScalarGridSpec(
        num_scalar_prefetch=1,
        grid=(nb, nsteps),
        in_specs=[per_req(sbw), per_req(sbw), per_req(sbw),
                  pl.BlockSpec(bias_rows.shape, lambda b, c, pt: (0, 0))]
                 + [page_spec(n) for n in range(pps)] * 2,
        out_specs=per_req(sbw),
        scratch_shapes=[pltpu.VMEM((LANES, sbw), F32), pltpu.VMEM((LANES, sbw), F32),
                        pltpu.VMEM((rows, sbw), F32), pltpu.VMEM((rows, SAMPLE_KEY_BLOCK), F32)],
    )
    return pl.pallas_call(
        functools.partial(_attn_sample_kernel, pages_per_step=pps),
        grid_spec=grid_spec,
        out_shape=jax.ShapeDtypeStruct((nb, tpos, sbw), F32),
        compiler_params=pltpu.CompilerParams(dimension_semantics=("arbitrary", "arbitrary"),
                                             vmem_limit_bytes=VMEM_LIMIT_BYTES),
        name="attn_sample",
    )(page_table.reshape(-1), q, k_new, v_new, bias_rows,
      *([feature_major(cache_k)] * pps), *([feature_major(cache_v)] * pps))


def _post_kernel(x_ref, a_ref, gm_ref, gan_ref, ggn_ref, wo_ref, gpost_ref, gpre2_ref,
                 w1_ref, w2_ref, gpost2_ref, y_ref, act_ref):
    hidden = w2_ref.shape[0]
    mix = jnp.concatenate([_rmsnorm(a_ref[...], gan_ref[...]),
                           _rmsnorm(gm_ref[...], ggn_ref[...])], axis=-1).astype(BF16)
    m = jnp.dot(mix, wo_ref[...], preferred_element_type=F32)
    h = x_ref[...] + _rmsnorm(m, gpost_ref[...])
    hn = _rmsnorm(h, gpre2_ref[...]).astype(BF16)
    for j in range(hidden // FFN_COLS):
        lo = j * FFN_COLS
        gate = jnp.dot(hn, w1_ref[:, lo:lo + FFN_COLS], preferred_element_type=F32)
        up = jnp.dot(hn, w1_ref[:, hidden + lo:hidden + lo + FFN_COLS], preferred_element_type=F32)
        act_ref[:, lo:lo + FFN_COLS] = (gate * jax.nn.sigmoid(gate) * up).astype(BF16)
    f = jnp.dot(act_ref[...], w2_ref[...], preferred_element_type=F32)
    y_ref[...] = h + _rmsnorm(f, gpost2_ref[...])


def _post(x, attn_o, gm_o, gan, ggn, wo, gpost, gpre2, w1, w2, gpost2):
    n, d = x.shape
    tm = min(PROJ_ROWS, n)
    hidden = w2.shape[0]
    row_spec = lambda w: pl.BlockSpec((tm, w), lambda i: (i, 0))
    const = lambda a: _resident(a.shape, lambda i: (0,) * a.ndim)
    return pl.pallas_call(
        _post_kernel,
        grid=(n // tm,),
        in_specs=[row_spec(d), row_spec(attn_o.shape[1]), row_spec(gm_o.shape[1]),
                  const(gan), const(ggn), const(wo), const(gpost), const(gpre2),
                  const(w1), const(w2), const(gpost2)],
        out_specs=row_spec(d),
        out_shape=jax.ShapeDtypeStruct((n, d), F32),
        scratch_shapes=[pltpu.VMEM((tm, hidden), BF16)],
        compiler_params=pltpu.CompilerParams(dimension_semantics=("arbitrary",),
                                             vmem_limit_bytes=VMEM_LIMIT_BYTES),
        name="post",
    )(x, attn_o, gm_o, gan, ggn, wo, gpost, gpre2, w1, w2, gpost2)


def kernel(x_prompt, x_sample, cache_k, cache_v, page_table, mix_pre_norm, w_in, sb_bias, gm_v_norm,
           gm_ws, gm_b, attn_out_norm, gm_out_norm, w_o, mix_post_norm, ffn_pre_norm, w_ffn_in,
           w_ffn_out, ffn_post_norm):
    depth = w_in.shape[0]
    b, s, d = x_prompt.shape
    nb, tpos, _ = x_sample.shape
    heads = cache_k.shape[3]
    sbw = heads * cache_k.shape[4]
    gmw = gm_v_norm.shape[1] * gm_v_norm.shape[2]
    groups = gm_ws.shape[1]
    q_scale = LOG2E / float(cache_k.shape[4]) ** 0.5
    row = lambda v: v.reshape(1, -1)

    xp, xs = x_prompt, x_sample.reshape(nb * tpos, d)
    kp_l, vp_l, ksn_l, vsn_l, gvs_l = [], [], [], [], []
    for l in range(depth):
        w = w_in[l]
        w_bf = w.astype(BF16)
        wqvt = w[:, :3 * sbw].T.astype(BF16)
        ws = gm_ws[l]
        tril = jnp.tril(jnp.ones(ws.shape[1:], ws.dtype))
        ws_tril = (ws * tril).astype(BF16)
        ws_open = (ws * tril)[:, :tpos, :tpos]
        ws_kron = jnp.stack([jnp.kron(jnp.eye(nb, dtype=ws.dtype), ws_open[g]) for g in range(groups)]).astype(BF16)
        bcol = jnp.tile(gm_b[l][:, :tpos].T, (nb, 1))
        bias2 = sb_bias[l] * LOG2E
        gvn = row(gm_v_norm[l])
        post = lambda x, a, g: _post(
            x, a, g, row(attn_out_norm[l]), row(gm_out_norm[l]), w_o[l].astype(BF16), row(mix_post_norm[l]),
            row(ffn_pre_norm[l]), w_ffn_in[l].astype(BF16), w_ffn_out[l].astype(BF16), row(ffn_post_norm[l]))

        k_p, v_p, kb, qt, vt, gm_p = _proj_prompt(
            xp, row(mix_pre_norm[l]), w_bf[:, sbw:], wqvt, gvn, ws_tril, gm_b[l].T, q_scale)
        attn_p = _attn_prompt(bias2, qt, kb, vt)
        xp = post(xp.reshape(b * s, d), attn_p.reshape(b * s, sbw), gm_p.reshape(b * s, gmw)).reshape(b, s, d)
        kp_l.append(jnp.transpose(k_p.reshape(b, heads, -1, s), (0, 3, 1, 2)))
        vp_l.append(jnp.transpose(v_p.reshape(b, heads, -1, s), (0, 3, 1, 2)))

        q_s, k_s, v_s, gvs, gm_s = _proj_sample(xs, row(mix_pre_norm[l]), w_bf, gvn, ws_kron, bcol, q_scale)
        attn_s = _attn_sample(
            l, page_table, q_s.reshape(nb, tpos, sbw), k_s.reshape(nb, tpos, sbw), v_s.reshape(nb, tpos, sbw),
            bias2, cache_k, cache_v)
        xs = post(xs, attn_s.reshape(nb * tpos, sbw), gm_s)
        ksn_l.append(k_s.reshape(nb, tpos, heads, -1))
        vsn_l.append(v_s.reshape(nb, tpos, heads, -1))
        gvs_l.append(gvs.reshape(nb, tpos, gmw))

    stack = lambda per_layer: per_layer[0][None] if depth == 1 else jnp.stack(per_layer)
    return (xp, xs.reshape(nb, tpos, d), stack(kp_l), stack(vp_l), stack(ksn_l), stack(vsn_l), stack(gvs_l))
```
